```python
import math
import jax, jax.numpy as jnp
from jax import lax
import numpy as np

D_MODEL = 1024
BATCH = 16
SEQ = 2048
DEPTH = 2

GRID_W = 64
CTX_LEN = 256
EPS = 1e-6
ROPE_BASE = 10000.0
F32 = jnp.float32

ML_HEADS = 4
ML_QK = 128
ML_V = 256
ML_CHUNK = 64
ML_QKW = ML_HEADS * ML_QK
ML_W = ML_HEADS * ML_V

SSD_HEADS = 16
SSD_P = 64
SSD_N = 64
SSD_GROUPS = 2
SSD_HPG = SSD_HEADS // SSD_GROUPS
SSD_CONV = 5
SSD_CHUNK = 64
SSD_W = SSD_HEADS * SSD_P
SSD_GN = SSD_GROUPS * SSD_N
SSD_XBC = SSD_W + 2 * SSD_GN

MLA_HEADS = 8
MLA_NOPE = 64
MLA_ROPE = 32
MLA_V = 64
MLA_Q_RANK = 384
MLA_KV_RANK = 256
MLA_SCALE = (MLA_NOPE + MLA_ROPE) ** -0.5
Q_BLOCK = 128

RW_HEADS = 16
RW_N = 64
RW_DECAY_RANK = 64
RW_A_RANK = 64
RW_GATE_RANK = 128
RW_W = RW_HEADS * RW_N

N_EXPERTS = 16
N_GROUPS = 4
EXPERTS_PER_GROUP = N_EXPERTS // N_GROUPS
TOP_K = 2
EXPERT_FF = 512

AB_SPLITS = (ML_QKW, ML_QKW, ML_W, ML_W, 2 * ML_HEADS, 2 * ML_HEADS, SSD_W, SSD_XBC, 2 * SSD_HEADS)
AB_IN = sum(AB_SPLITS)
AB_OUT = ML_W + SSD_W
MLA_SPLITS = (MLA_Q_RANK, MLA_KV_RANK, MLA_ROPE)
MLA_IN = sum(MLA_SPLITS)
RW_SPLITS = (RW_W, RW_W, RW_W, 2 * RW_DECAY_RANK, 2 * RW_A_RANK, RW_GATE_RANK)
RW_IN = sum(RW_SPLITS)
CD_IN = MLA_IN + RW_IN
CD_OUT = MLA_HEADS * MLA_V + RW_W

kernel_name = "hybrid_mlstm_ssd_mla_rwkv7_grouped_moe_dit"


def rmsnorm(x, w):
    xf = x.astype(F32)
    y = xf * lax.rsqrt(jnp.mean(xf * xf, axis=-1, keepdims=True) + EPS)
    return y.astype(x.dtype) * w


def head_rmsnorm(h, w):
    hf = h.astype(F32)
    y = hf * lax.rsqrt(jnp.mean(hf * hf, axis=-1, keepdims=True) + EPS)
    return y.reshape(*h.shape[:2], -1) * w


def split_cols(p, sizes):
    return jnp.split(p, [int(i) for i in np.cumsum(sizes)[:-1]], axis=-1)


def to_heads(t, n):
    return t.reshape(*t.shape[:-1], n, t.shape[-1] // n)


def to_chunks(t, L):
    B, T = t.shape[:2]
    return jnp.moveaxis(t.reshape(B, T // L, L, *t.shape[2:]), 1, 0)


def from_chunks(t):
    nc, B, L = t.shape[:3]
    return jnp.moveaxis(t, 0, 1).reshape(B, nc * L, *t.shape[3:])


def dwconv_centred(x, w, b):
    y = lax.conv_general_dilated(x, w[:, None, :].astype(x.dtype), (1,), 'SAME',
                                 dimension_numbers=('NWC', 'WIO', 'NWC'),
                                 feature_group_count=x.shape[-1])
    return y + b


def centred_shift(x):
    prev = jnp.pad(x, ((0, 0), (1, 0), (0, 0)))[:, :-1]
    nxt = jnp.pad(x, ((0, 0), (0, 1), (0, 0)))[:, 1:]
    return 0.5 * (prev + nxt)


def axial_angles(n_rows):
    row = jnp.repeat(jnp.arange(n_rows, dtype=F32), GRID_W)
    col = jnp.tile(jnp.arange(GRID_W, dtype=F32), n_rows)
    n_freq = MLA_ROPE // 4
    inv = ROPE_BASE ** (-jnp.arange(n_freq, dtype=F32) / n_freq)
    return jnp.concatenate([row[:, None] * inv, col[:, None] * inv], axis=-1)


def apply_rope(x, ang):
    x1, x2 = jnp.split(x, 2, axis=-1)
    cos, sin = jnp.cos(ang).astype(x.dtype), jnp.sin(ang).astype(x.dtype)
    return jnp.concatenate([x1 * cos - x2 * sin, x1 * sin + x2 * cos], axis=-1)


def attend_blocked(q, k, v):
    B, S, H, d = q.shape
    qb = jnp.moveaxis(q.reshape(B, S // Q_BLOCK, Q_BLOCK, H, d), 1, 0)

    def block(qi):
        s = jnp.einsum('bqhd,bkhd->bhqk', qi, k).astype(F32)
        p = jax.nn.softmax(s, axis=-1).astype(v.dtype)
        return jnp.einsum('bhqk,bkhv->bqhv', p, v)

    return from_chunks(lax.map(block, qb))


def prefixed_bidir(scan_fn, ctx_fwd, lat_fwd, ctx_bwd, lat_bwd, init):
    flip = lambda ts: tuple(jnp.flip(t, axis=1) for t in ts)
    yc_f, st_f = scan_fn(*ctx_fwd, init)
    yl_f, _ = scan_fn(*lat_fwd, st_f)
    yc_b, st_b = scan_fn(*flip(ctx_bwd), init)
    yl_b, _ = scan_fn(*flip(lat_bwd), st_b)
    return yc_f + jnp.flip(yc_b, axis=1), yl_f + jnp.flip(yl_b, axis=1)


def mlstm_chunked(q, k, v, ig, lf, state):
    L = ML_CHUNK
    mask = jnp.tril(jnp.ones((L, L), bool))[None, :, :, None]

    def step(carry, inp):
        C, n, m = carry
        qc, kc, vc, ic, fc = inp
        b = jnp.cumsum(fc, axis=1)
        logD = jnp.where(mask, b[:, :, None, :] - b[:, None, :, :] + ic[:, None, :, :], -jnp.inf)
        inter = b + m[:, None, :]
        m_j = jnp.maximum(inter, jnp.max(logD, axis=2))
        s = jnp.einsum('bjhd,bshd->bjsh', qc, kc) * jnp.exp(logD - m_j[:, :, None, :])
        w_int = jnp.exp(inter - m_j)
        num = jnp.einsum('bjsh,bshv->bjhv', s, vc) + w_int[..., None] * jnp.einsum('bhvd,bjhd->bjhv', C, qc)
        den = jnp.sum(s, axis=2) + w_int * jnp.einsum('bhd,bjhd->bjh', n, qc)
        h = num / jnp.maximum(jnp.abs(den), jnp.exp(-m_j))[..., None]
        logE = b[:, -1:, :] - b + ic
        m_new = jnp.maximum(b[:, -1] + m, jnp.max(logE, axis=1))
        E = jnp.exp(logE - m_new[:, None, :])
        dec = jnp.exp(b[:, -1] + m - m_new)
        C = dec[..., None, None] * C + jnp.einsum('bsh,bshv,bshd->bhvd', E, vc, kc)
        n = dec[..., None] * n + jnp.einsum('bsh,bshd->bhd', E, kc)
        return (C, n, m_new), h

    state, hs = lax.scan(step, state, tuple(to_chunks(t, L) for t in (q, k, v, ig, lf)))
    return from_chunks(hs), state


def ssd_chunked(x, dt, la, bm, cm, state):
    B, T = x.shape[:2]
    L = SSD_CHUNK
    x = x.reshape(B, T, SSD_GROUPS, SSD_HPG, SSD_P)
    dt = dt.reshape(B, T, SSD_GROUPS, SSD_HPG)
    la = la.reshape(B, T, SSD_GROUPS, SSD_HPG)
    mask = jnp.tril(jnp.ones((L, L), bool))[None, :, :, None, None]

    def step(S, inp):
        xc, dtc, lac, bc, cc = inp
        b = jnp.cumsum(lac, axis=1)
        seg = jnp.exp(jnp.where(mask, b[:, :, None] - b[:, None, :], -jnp.inf))
        cb = jnp.einsum('bjgn,bsgn->bjsg', cc, bc)
        w = cb[..., None] * seg * dtc[:, None]
        y = jnp.einsum('bjsgh,bsghp->bjghp', w, xc) + jnp.exp(b)[..., None] * jnp.einsum('bghpn,bjgn->bjghp', S, cc)
        e = jnp.exp(b[:, -1:] - b) * dtc
        S = jnp.exp(b[:, -1])[..., None, None] * S + jnp.einsum('bsgh,bsghp,bsgn->bghpn', e, xc, bc)
        return S, y

    state, ys = lax.scan(step, state, tuple(to_chunks(t, L) for t in (x, dt, la, bm, cm)))
    return from_chunks(ys).reshape(B, T, SSD_HEADS, SSD_P), state


def rwkv7_scan(r, w, k, v, a, b, state):
    def step(S, inp):
        r_t, w_t, k_t, v_t, a_t, b_t = inp
        sa = jnp.einsum('bhvk,bhk->bhv', S, a_t)
        S = S * w_t[:, :, None, :] + sa[..., None] * b_t[:, :, None, :] + v_t[..., None] * k_t[:, :, None, :]
        return S, jnp.einsum('bhvk,bhk->bhv', S, r_t)

    state, ys = lax.scan(step, state, tuple(jnp.moveaxis(t, 1, 0) for t in (r, w, k, v, a, b)))
    return jnp.moveaxis(ys, 0, 1), state


def mixer_ab(u_c, u_l, w_in, w_out, ml_i_bias, ml_f_bias, ml_norm, ssd_conv_w, ssd_conv_b,
             ssd_dt_bias, ssd_a_log, ssd_d, ssd_norm, with_ctx_out):
    Bsz = u_l.shape[0]
    p_c = split_cols(u_c @ w_in, AB_SPLITS)
    p_l = split_cols(u_l @ w_in, AB_SPLITS)

    def ml_dirs(p):
        q = to_heads(p[0].astype(F32), ML_HEADS) * ML_QK ** -0.5
        k = to_heads(p[1].astype(F32), ML_HEADS)
        v = to_heads(p[2].astype(F32), ML_HEADS)
        ig = to_heads(p[4].astype(F32), 2)
        fg = to_heads(p[5].astype(F32), 2)
        return [(q, k, v, ig[..., d, :] + ml_i_bias[d], jax.nn.log_sigmoid(fg[..., d, :] + ml_f_bias[d])) for d in range(2)]

    mc_f, mc_b = ml_dirs(p_c)
    mlf, mlb = ml_dirs(p_l)
    ml_init = (jnp.zeros((Bsz, ML_HEADS, ML_V, ML_QK), F32), jnp.zeros((Bsz, ML_HEADS, ML_QK), F32),
               jnp.zeros((Bsz, ML_HEADS), F32))
    h_c, h_l = prefixed_bidir(mlstm_chunked, mc_f, mlf, mc_b, mlb, ml_init)

    def ml_out(h, p):
        return head_rmsnorm(h, ml_norm) * jax.nn.sigmoid(p[3].astype(F32))

    def ssd_dirs(p):
        xbc = jax.nn.silu(dwconv_centred(p[7], ssd_conv_w, ssd_conv_b)).astype(F32)
        xs, bm, cm = split_cols(xbc, (SSD_W, SSD_GN, SSD_GN))
        xs, bm, cm = to_heads(xs, SSD_HEADS), to_heads(bm, SSD_GROUPS), to_heads(cm, SSD_GROUPS)
        dt_raw = to_heads(p[8].astype(F32), 2)
        out = []
        for d in range(2):
            dt = jax.nn.softplus(dt_raw[..., d, :] + ssd_dt_bias[d])
            out.append((xs, dt, -dt * jnp.exp(ssd_a_log[d]), bm, cm))
        return out

    sc_f, sc_b = ssd_dirs(p_c)
    sl_f, sl_b = ssd_dirs(p_l)
    ssd_init = jnp.zeros((Bsz, SSD_GROUPS, SSD_HPG, SSD_P, SSD_N), F32)
    y_c, y_l = prefixed_bidir(ssd_chunked, sc_f, sl_f, sc_b, sl_b, ssd_init)

    def ssd_out(y, xs, p):
        y = (y + ssd_d[:, None] * xs).reshape(*y.shape[:2], SSD_W)
        return rmsnorm(y * jax.nn.silu(p[6].astype(F32)), ssd_norm)

    mix_l = jnp.concatenate([ml_out(h_l, p_l), ssd_out(y_l, sl_f[0], p_l)], axis=-1).astype(u_l.dtype) @ w_out
    mix_c = None
    if with_ctx_out:
        mix_c = jnp.concatenate([ml_out(h_c, p_c), ssd_out(y_c, sc_f[0], p_c)], axis=-1).astype(u_c.dtype) @ w_out
    return mix_c, mix_l


def mixer_cd(u_c, u_l, ang, w_in, w_out, mla_qn, mla_w_uq, mla_kvn, mla_w_ukv, rw_mu, rw_w0, rw_w2,
             rw_a0, rw_a2, rw_g2, rw_kk, rw_ka, rw_rk, rw_ln, with_ctx_out):
    Bsz, S = u_l.shape[:2]

    def project(u):
        p = u @ w_in
        p_mla, p_rw = p[..., :MLA_IN], p[..., MLA_IN:].astype(F32)
        p_rw = p_rw + rw_mu * (centred_shift(p_rw) - p_rw)
        return split_cols(p_mla, MLA_SPLITS), split_cols(p_rw, RW_SPLITS)

    (m_c, r_c), (m_l, r_l) = project(u_c), project(u_l)

    def mla_q(m, rotate):
        q = to_heads(rmsnorm(m[0], mla_qn) @ mla_w_uq, MLA_HEADS)
        q_nope, q_rope = q[..., :MLA_NOPE], q[..., MLA_NOPE:]
        if rotate:
            q_rope = apply_rope(q_rope, ang[:, None, :])
        return jnp.concatenate([q_nope, q_rope], axis=-1) * MLA_SCALE

    def mla_kv(m, rotate):
        kv = to_heads(rmsnorm(m[1], mla_kvn) @ mla_w_ukv, MLA_HEADS)
        k_nope, v = kv[..., :MLA_NOPE], kv[..., MLA_NOPE:]
        kr = apply_rope(m[2], ang) if rotate else m[2]
        k_rope = jnp.broadcast_to(kr[:, :, None, :], (*kr.shape[:2], MLA_HEADS, MLA_ROPE))
        return jnp.concatenate([k_nope, k_rope], axis=-1), v

    k_c, v_c = mla_kv(m_c, False)
    k_l, v_l = mla_kv(m_l, True)
    att_l = attend_blocked(mla_q(m_l, True), jnp.concatenate([k_c, k_l], axis=1),
                           jnp.concatenate([v_c, v_l], axis=1))

    def rw_dirs(p):
        r, k, v, w_low, a_low, _ = p
        kk = to_heads(k * rw_kk, RW_HEADS)
        kk = kk / jnp.maximum(jnp.sqrt(jnp.sum(kk * kk, axis=-1, keepdims=True)), 1e-12)
        w_low, a_low = to_heads(w_low, 2), to_heads(a_low, 2)
        out = []
        for d in range(2):
            w_log = -jax.nn.softplus(-(rw_w0[d] + jnp.tanh(w_low[..., d, :]) @ rw_w2[d])) - 0.5
            a = jax.nn.sigmoid(rw_a0[d] + a_low[..., d, :] @ rw_a2[d])
            k_d = k * (1 + (a - 1) * rw_ka)
            out.append((to_heads(r, RW_HEADS), to_heads(jnp.exp(-jnp.exp(w_log)), RW_HEADS),
                        to_heads(k_d, RW_HEADS), to_heads(v, RW_HEADS), -kk, kk * to_heads(a, RW_HEADS)))
        return out

    rc_f, rc_b = rw_dirs(r_c)
    rl_f, rl_b = rw_dirs(r_l)
    rw_init = jnp.zeros((Bsz, RW_HEADS, RW_N, RW_N), F32)
    o_c, o_l = prefixed_bidir(rwkv7_scan, rc_f, rl_f, rc_b, rl_b, rw_init)

    def rw_out(o, p, dirs):
        bonus = sum(jnp.sum(dd[0] * dd[2] * rw_rk, axis=-1, keepdims=True) * dd[3] for dd in dirs)
        y = head_rmsnorm(o, rw_ln) + bonus.reshape(*o.shape[:2], RW_W)
        return y * (jax.nn.sigmoid(p[5]) @ rw_g2)

    mix_l = jnp.concatenate([att_l.reshape(Bsz, S, -1), rw_out(o_l, r_l, (rl_f, rl_b))],
                            axis=-1).astype(u_l.dtype) @ w_out
    mix_c = None
    if with_ctx_out:
        att_c = attend_blocked(mla_q(m_c, False), k_c, v_c)
        mix_c = jnp.concatenate([att_c.reshape(Bsz, u_c.shape[1], -1), rw_out(o_c, r_c, (rc_f, rc_b))],
                                axis=-1).astype(u_c.dtype) @ w_out
    return mix_c, mix_l


def swiglu(x, wg, wu, wd):
    return (jax.nn.silu(x @ wg) * (x @ wu)) @ wd


def moe(v, router_w, router_bias, w_gate, w_up, w_down, s_gate, s_up, s_down):
    B, T, D = v.shape
    xt = v.reshape(B * T, D)
    aff = jax.nn.sigmoid((xt @ router_w).astype(F32))
    sel = (aff + router_bias).reshape(-1, N_GROUPS, EXPERTS_PER_GROUP)
    g_idx = jnp.argmax(jnp.sum(lax.top_k(sel, TOP_K)[0], axis=-1), axis=-1)
    in_group = jnp.take_along_axis(sel, g_idx[:, None, None], axis=1)[:, 0]
    _, local = lax.top_k(in_group, TOP_K)
    e_idx = g_idx[:, None] * EXPERTS_PER_GROUP + local
    w = jnp.take_along_axis(aff, e_idx, axis=-1)
    w = w / jnp.sum(w, axis=-1, keepdims=True)
    gates = jnp.einsum('nk,nke->ne', w, jax.nn.one_hot(e_idx, N_EXPERTS, dtype=F32)).astype(xt.dtype)
    out = swiglu(xt, s_gate, s_up, s_down)
    for e in range(N_EXPERTS):
        out = out + gates[:, e:e + 1] * swiglu(xt, w_gate[e], w_up[e], w_down[e])
    return out.reshape(B, T, D)


def setup_inputs(seed: int = 0) -> dict:
    key = jax.random.key(seed)
    keys = jax.random.split(key, 64)
    counter = [0]

    def nxt():
        k = keys[counter[0]]
        counter[0] += 1
        return k

    def nrm(shape, scale=1.0):
        return jax.random.normal(nxt(), shape, F32) * scale

    def unif(shape, lo, hi):
        return jax.random.uniform(nxt(), shape, F32, lo, hi)

    D = D_MODEL
    NE, NO = (DEPTH + 1) // 2, DEPTH // 2
    dt0 = jnp.exp(unif((NE, 2, SSD_HEADS), math.log(1e-3), math.log(1e-1)))
    return {
        "x": nrm((BATCH, SEQ, D)),
        "c": nrm((BATCH, D)),
        "ctx": nrm((BATCH, CTX_LEN, D)),
        "c_ctx": nrm((D,)),
        "w_mod": nrm((DEPTH, D, 6 * D), 0.5 * D ** -0.5),
        "b_mod": nrm((DEPTH, 6 * D), 0.02),
        "norm_g": 1.0 + nrm((DEPTH, 4, D), 0.05),
        "ab_w_in": nrm((NE, D, AB_IN), D ** -0.5),
        "ab_w_out": nrm((NE, AB_OUT, D), AB_OUT ** -0.5),
        "ml_i_bias": nrm((NE, 2, ML_HEADS), 0.1) - 1.0,
        "ml_f_bias": unif((NE, 2, ML_HEADS), 3.0, 6.0),
        "ml_norm": 1.0 + nrm((NE, ML_W), 0.05),
        "ssd_conv_w": nrm((NE, SSD_CONV, SSD_XBC), SSD_CONV ** -0.5),
        "ssd_conv_b": nrm((NE, SSD_XBC), 0.02),
        "ssd_dt_bias": dt0 + jnp.log(-jnp.expm1(-dt0)),
        "ssd_a_log": jnp.log(unif((NE, 2, SSD_HEADS), 1.0, 16.0)),
        "ssd_d": 1.0 + nrm((NE, SSD_HEADS), 0.1),
        "ssd_norm": 1.0 + nrm((NE, SSD_W), 0.05),
        "cd_w_in": nrm((NO, D, CD_IN), D ** -0.5),
        "cd_w_out": nrm((NO, CD_OUT, D), CD_OUT ** -0.5),
        "mla_qn": 1.0 + nrm((NO, MLA_Q_RANK), 0.05),
        "mla_w_uq": nrm((NO, MLA_Q_RANK, MLA_HEADS * (MLA_NOPE + MLA_ROPE)), MLA_Q_RANK ** -0.5),
        "mla_kvn": 1.0 + nrm((NO, MLA_KV_RANK), 0.05),
        "mla_w_ukv": nrm((NO, MLA_KV_RANK, MLA_HEADS * (MLA_NOPE + MLA_V)), MLA_KV_RANK ** -0.5),
        "rw_mu": unif((NO, RW_IN), 0.0, 1.0),
        "rw_w0": unif((NO, 2, RW_W), -6.0, -1.0),
        "rw_w2": nrm((NO, 2, RW_DECAY_RANK, RW_W), 0.1),
        "rw_a0": nrm((NO, 2, RW_W), 0.1),
        "rw_a2": nrm((NO, 2, RW_A_RANK, RW_W), 0.1),
        "rw_g2": nrm((NO, RW_GATE_RANK, RW_W), RW_GATE_RANK ** -0.5),
        "rw_kk": 0.85 + nrm((NO, RW_W), 0.05),
        "rw_ka": 1.0 + nrm((NO, RW_W), 0.05),
        "rw_rk": nrm((NO, RW_HEADS, RW_N), 0.1),
        "rw_ln": 1.0 + nrm((NO, RW_W), 0.05),
        "router_w": nrm((D, N_EXPERTS), D ** -0.5),
        "router_bias": nrm((N_EXPERTS,), 0.01),
        "exp_w_gate": nrm((DEPTH, N_EXPERTS, D, EXPERT_FF), D ** -0.5),
        "exp_w_up": nrm((DEPTH, N_EXPERTS, D, EXPERT_FF), D ** -0.5),
        "exp_w_down": nrm((DEPTH, N_EXPERTS, EXPERT_FF, D), EXPERT_FF ** -0.5),
        "sh_w_gate": nrm((DEPTH, D, EXPERT_FF), D ** -0.5),
        "sh_w_up": nrm((DEPTH, D, EXPERT_FF), D ** -0.5),
        "sh_w_down": nrm((DEPTH, EXPERT_FF, D), EXPERT_FF ** -0.5),
    }


def reference(x, c, ctx, c_ctx, w_mod, b_mod, norm_g, ab_w_in, ab_w_out, ml_i_bias, ml_f_bias, ml_norm,
              ssd_conv_w, ssd_conv_b, ssd_dt_bias, ssd_a_log, ssd_d, ssd_norm, cd_w_in, cd_w_out, mla_qn,
              mla_w_uq, mla_kvn, mla_w_ukv, rw_mu, rw_w0, rw_w2, rw_a0, rw_a2, rw_g2, rw_kk, rw_ka, rw_rk,
              rw_ln, router_w, router_bias, exp_w_gate, exp_w_up, exp_w_down, sh_w_gate, sh_w_up, sh_w_down):
    n_rows = x.shape[1] // GRID_W
    ang = axial_angles(n_rows)
    s_lat = jax.nn.silu(c)[:, None, :]
    s_ctx = jax.nn.silu(c_ctx)
    h_l, h_c = x, ctx
    for l in range(DEPTH):
        last = l == DEPTH - 1
        sh_a, sc_a, g_a, sh_f, sc_f, g_f = jnp.split(s_lat @ w_mod[l] + b_mod[l], 6, axis=-1)
        csh_a, csc_a, cg_a, csh_f, csc_f, cg_f = jnp.split(s_ctx @ w_mod[l] + b_mod[l], 6, axis=-1)
        u_l = rmsnorm(h_l, norm_g[l, 0]) * (1 + sc_a) + sh_a
        u_c = rmsnorm(h_c, norm_g[l, 0]) * (1 + csc_a) + csh_a
        i = l // 2
        if l % 2 == 0:
            y_c, y_l = mixer_ab(u_c, u_l, ab_w_in[i], ab_w_out[i], ml_i_bias[i], ml_f_bias[i], ml_norm[i],
                                ssd_conv_w[i], ssd_conv_b[i], ssd_dt_bias[i], ssd_a_log[i], ssd_d[i],
                                ssd_norm[i], not last)
        else:
            y_c, y_l = mixer_cd(u_c, u_l, ang, cd_w_in[i], cd_w_out[i], mla_qn[i], mla_w_uq[i], mla_kvn[i],
                                mla_w_ukv[i], rw_mu[i], rw_w0[i], rw_w2[i], rw_a0[i], rw_a2[i], rw_g2[i],
                                rw_kk[i], rw_ka[i], rw_rk[i], rw_ln[i], not last)
        h_l = h_l + g_a * rmsnorm(y_l, norm_g[l, 1])
        v_l = rmsnorm(h_l, norm_g[l, 2]) * (1 + sc_f) + sh_f
        moe_w = (router_w, router_bias, exp_w_gate[l], exp_w_up[l], exp_w_down[l], sh_w_gate[l], sh_w_up[l], sh_w_down[l])
        if last:
            f_l = moe(v_l, *moe_w)
        else:
            h_c = h_c + cg_a * rmsnorm(y_c, norm_g[l, 1])
            v_c = rmsnorm(h_c, norm_g[l, 2]) * (1 + csc_f) + csh_f
            f = moe(jnp.concatenate([v_c, v_l], axis=1), *moe_w)
            f_c, f_l = f[:, :v_c.shape[1]], f[:, v_c.shape[1]:]
            h_c = h_c + cg_f * rmsnorm(f_c, norm_g[l, 3])
        h_l = h_l + g_f * rmsnorm(f_l, norm_g[l, 3])
    return h_l
```

```python
import functools

import jax
import jax.numpy as jnp
from jax import lax
from jax.experimental import pallas as pl
from jax.experimental.pallas import tpu as pltpu

F32 = jnp.float32
BF16 = jnp.bfloat16
HIGHEST = lax.Precision.HIGHEST

D_MODEL = 1024
CTX_LEN = 256
GRID_W = 64
EPS = 1e-6
ROPE_BASE = 10000.0
TOK_BLK = 256
CHUNK = 64
LANES = 128

ML_HEADS, ML_QK, ML_V = 4, 128, 256
ML_QKW, ML_W = ML_HEADS * ML_QK, ML_HEADS * ML_V
SSD_HEADS, SSD_P, SSD_N, SSD_GROUPS, SSD_CONV = 16, 64, 64, 2, 5
SSD_HPG = SSD_HEADS // SSD_GROUPS
SSD_W, SSD_GN = SSD_HEADS * SSD_P, SSD_GROUPS * SSD_N
SSD_XBC = SSD_W + 2 * SSD_GN
MLA_HEADS, MLA_NOPE, MLA_ROPE, MLA_V = 8, 64, 32, 64
MLA_Q_RANK, MLA_KV_RANK = 384, 256
MLA_SCALE = (MLA_NOPE + MLA_ROPE) ** -0.5
RW_HEADS, RW_N = 16, 64
RW_W = RW_HEADS * RW_N
RW_DECAY_RANK, RW_A_RANK, RW_GATE_RANK = 64, 64, 128
N_EXPERTS, N_GROUPS, EXPERT_FF = 16, 4, 512
EXPERTS_PER_GROUP = N_EXPERTS // N_GROUPS

VMEM_LIMIT = 56 * 1024 * 1024


def _cparams(*sem):
    return pltpu.CompilerParams(dimension_semantics=sem, vmem_limit_bytes=VMEM_LIMIT)


def _silu(x):
    return x * jax.nn.sigmoid(x)


def _softplus(x):
    return jnp.maximum(x, 0.0) + jnp.log1p(jnp.exp(-jnp.abs(x)))


def _log_sigmoid(x):
    return jnp.minimum(x, 0.0) - jnp.log1p(jnp.exp(-jnp.abs(x)))


def _rms(x):
    return x * lax.rsqrt(jnp.mean(x * x, axis=-1, keepdims=True) + EPS)


def _dot_t(a, b):
    return lax.dot_general(a, b, (((1,), (1,)), ((), ())), preferred_element_type=F32)


def _dot(a, b):
    return jnp.dot(a, b, preferred_element_type=F32)


def _tri_masks(d):
    row = lax.broadcasted_iota(jnp.int32, (CHUNK, CHUNK), 0)
    col = lax.broadcasted_iota(jnp.int32, (CHUNK, CHUNK), 1)
    incl = (col <= row) if d == 0 else (col >= row)
    strict = (col < row) if d == 0 else (col > row)
    return incl, strict


def _mod_kernel(s_ref, w_ref, b_ref, o_ref):
    s = _silu(s_ref[...])
    o_ref[...] = jnp.dot(s, w_ref[...], precision=HIGHEST, preferred_element_type=F32) + b_ref[...]


def _modulation(c, c_ctx, w, b):
    bsz = c.shape[0]
    rows = 8 * ((bsz + 1 + 7) // 8)
    s = jnp.zeros((rows, D_MODEL), F32).at[:bsz].set(c).at[bsz].set(c_ctx)
    n = w.shape[1]
    return pl.pallas_call(
        _mod_kernel,
        grid=(n // D_MODEL,),
        in_specs=[pl.BlockSpec((rows, D_MODEL), lambda j: (0, 0)),
                  pl.BlockSpec((D_MODEL, D_MODEL), lambda j: (0, j)),
                  pl.BlockSpec((1, D_MODEL), lambda j: (0, j))],
        out_specs=pl.BlockSpec((rows, D_MODEL), lambda j: (0, j)),
        out_shape=jax.ShapeDtypeStruct((rows, n), F32),
        compiler_params=_cparams("arbitrary"),
        name="modulation",
    )(s, w, b.reshape(1, n))


def _block_tables(mod, norm_g, bsz, blocks_per_row, with_ctx):
    j = jnp.arange(blocks_per_row)
    b = jnp.arange(bsz)
    if with_ctx:
        sel = jnp.where(j[None, :] == 0, bsz, b[:, None]).reshape(-1)
    else:
        sel = jnp.broadcast_to(b[:, None], (bsz, blocks_per_row)).reshape(-1)
    m = mod[sel]
    sh_a, sc_a, g_a, sh_f, sc_f, g_f = jnp.split(m, 6, axis=-1)
    tabs = (norm_g[0] * (1 + sc_a), sh_a, g_a * norm_g[1], norm_g[2] * (1 + sc_f), sh_f, g_f * norm_g[3])
    return tuple(t[:, None, :] for t in tabs)


def _norm_proj_kernel(x_ref, a_ref, b_ref, w_ref, *refs, segs, nsub):
    out_refs, u_ref = refs[:len(segs)], refs[len(segs)]
    for s in range(nsub):
        rows = slice(s * TOK_BLK, (s + 1) * TOK_BLK)
        u_ref[rows, :] = (_rms(x_ref[rows, :]) * a_ref[s] + b_ref[s]).astype(BF16)
    u = u_ref[...]
    for o_ref, (start, width) in zip(out_refs, segs):
        o_ref[...] = _dot(u, w_ref[:, start:start + width]).astype(o_ref.dtype)


def _norm_proj(x, a_tab, b_tab, w, segs, dtypes, name):
    rows = x.shape[0]
    nsub = 2
    tm = nsub * TOK_BLK
    assert rows % tm == 0
    n = w.shape[1]
    return pl.pallas_call(
        functools.partial(_norm_proj_kernel, segs=segs, nsub=nsub),
        grid=(rows // tm,),
        in_specs=[pl.BlockSpec((tm, D_MODEL), lambda i: (i, 0)),
                  pl.BlockSpec((nsub, 1, D_MODEL), lambda i: (i, 0, 0)),
                  pl.BlockSpec((nsub, 1, D_MODEL), lambda i: (i, 0, 0)),
                  pl.BlockSpec((D_MODEL, n), lambda i: (0, 0))],
        out_specs=[pl.BlockSpec((tm, wd), lambda i: (i, 0)) for _, wd in segs],
        out_shape=[jax.ShapeDtypeStruct((rows, wd), dt) for (_, wd), dt in zip(segs, dtypes)],
        scratch_shapes=[pltpu.VMEM((tm, D_MODEL), BF16)],
        compiler_params=_cparams("arbitrary"),
        name=name,
    )(x, a_tab, b_tab, w)


def _chunk_fwd(i):
    return i


def _chunk_bwd(i, n_ctx, n_all):
    return jnp.where(i < n_ctx, n_ctx - 1 - i, n_all + n_ctx - 1 - i)


def _mlstm_kernel(qkv_f, g_f, qkv_b, g_b, bias_ref, hf_ref, hb_ref, c_ref, n_ref, m_ref):
    @pl.when(pl.program_id(1) == 0)
    def _():
        c_ref[...] = jnp.zeros_like(c_ref)
        n_ref[...] = jnp.zeros_like(n_ref)
        m_ref[...] = jnp.zeros_like(m_ref)

    scale = ML_QK ** -0.5
    for d, (qkv_ref, g_ref, o_ref) in enumerate(((qkv_f, g_f, hf_ref), (qkv_b, g_b, hb_ref))):
        incl, _ = _tri_masks(d)
        pre = g_ref[0] + bias_ref[...]
        bcum = jnp.dot(incl.astype(F32), _log_sigmoid(pre), precision=HIGHEST, preferred_element_type=F32)
        pre_t, bcum_t = pre.T, bcum.T
        last = CHUNK - 1 if d == 0 else 0
        for h in range(ML_HEADS):
            ci = d * ML_HEADS + h
            cf = 2 * ML_HEADS + ci
            b_col, b_row = bcum[:, cf:cf + 1], bcum_t[cf:cf + 1, :]
            ig_col, ig_row = pre[:, ci:ci + 1], pre_t[ci:ci + 1, :]
            btot = bcum[last:last + 1, cf:cf + 1]
            m_old = m_ref[ci:ci + 1, 0:1]
            log_d = jnp.where(incl, b_col - b_row + ig_row, -jnp.inf)
            inter = b_col + m_old
            m_j = jnp.maximum(inter, jnp.max(log_d, axis=1, keepdims=True))
            q = qkv_ref[0, :, h * ML_QK:(h + 1) * ML_QK]
            k = qkv_ref[0, :, ML_QKW + h * ML_QK:ML_QKW + (h + 1) * ML_QK]
            v = qkv_ref[0, :, 2 * ML_QKW + h * ML_V:2 * ML_QKW + (h + 1) * ML_V]
            s = _dot_t(q, k) * scale * jnp.exp(log_d - m_j)
            w_int = jnp.exp(inter - m_j)
            c_old = c_ref[ci]
            n_old = n_ref[ci]
            cq = _dot_t(q, c_old.astype(BF16)) * scale
            nq = jnp.sum(q.astype(F32) * n_old, axis=1, keepdims=True) * scale
            num = _dot(s.astype(BF16), v) + w_int * cq
            den = jnp.sum(s, axis=1, keepdims=True) + w_int * nq
            o_ref[0, :, h * ML_V:(h + 1) * ML_V] = num / jnp.maximum(jnp.abs(den), jnp.exp(-m_j))
            log_e_row = btot - b_row + ig_row
            log_e_col = btot - b_col + ig_col
            m_new = jnp.maximum(btot + m_old, jnp.max(log_e_row, axis=1, keepdims=True))
            e_col = jnp.exp(log_e_col - m_new)
            dec = jnp.exp(btot + m_old - m_new)
            ve_t = (v.astype(F32) * e_col).T.astype(BF16)
            c_ref[ci] = dec * c_old + _dot(ve_t, k)
            n_ref[ci] = dec * n_old + jnp.sum(e_col * k.astype(F32), axis=0, keepdims=True)
            m_ref[ci:ci + 1, :] = jnp.broadcast_to(m_new, (1, LANES))


def _mlstm(qkv, gates, bias_row):
    bsz, t, _ = qkv.shape
    nc, nctx = t // CHUNK, CTX_LEN // CHUNK
    bwd = functools.partial(_chunk_bwd, n_ctx=nctx, n_all=nc)
    w = qkv.shape[2]
    return pl.pallas_call(
        _mlstm_kernel,
        grid=(bsz, nc),
        in_specs=[pl.BlockSpec((1, CHUNK, w), lambda b, i: (b, i, 0)),
                  pl.BlockSpec((1, CHUNK, LANES), lambda b, i: (b, i, 0)),
                  pl.BlockSpec((1, CHUNK, w), lambda b, i: (b, bwd(i), 0)),
                  pl.BlockSpec((1, CHUNK, LANES), lambda b, i: (b, bwd(i), 0)),
                  pl.BlockSpec((1, LANES), lambda b, i: (0, 0))],
        out_specs=[pl.BlockSpec((1, CHUNK, ML_W), lambda b, i: (b, i, 0)),
                   pl.BlockSpec((1, CHUNK, ML_W), lambda b, i: (b, bwd(i), 0))],
        out_shape=[jax.ShapeDtypeStruct((bsz, t, ML_W), F32)] * 2,
        scratch_shapes=[pltpu.VMEM((2 * ML_HEADS, ML_V, ML_QK), F32),
                        pltpu.VMEM((2 * ML_HEADS, 1, ML_QK), F32),
                        pltpu.VMEM((2 * ML_HEADS, LANES), F32)],
        compiler_params=_cparams("arbitrary", "arbitrary"),
        name="mlstm_scan",
    )(qkv, gates, qkv, gates, bias_row)


def _conv_kernel(x_ref, w_ref, b_ref, o_ref):
    x = x_ref[0]
    t_len = x.shape[0]
    t = lax.broadcasted_iota(jnp.int32, (t_len, 1), 0)
    lo = jnp.where(t < CTX_LEN, 0, CTX_LEN)
    hi = jnp.where(t < CTX_LEN, CTX_LEN, t_len)
    half = SSD_CONV // 2
    acc = x * w_ref[half:half + 1, :]
    for tap in range(SSD_CONV):
        d = tap - half
        if d == 0:
            continue
        shifted = pltpu.roll(x, shift=(-d) % t_len, axis=0)
        ok = (t + d >= lo) & (t + d < hi)
        acc = acc + jnp.where(ok, shifted, 0.0) * w_ref[tap:tap + 1, :]
    o_ref[0] = _silu(acc + b_ref[...])


def _ssd_conv(xbc, w, b):
    bsz, t, n = xbc.shape
    tn = 256
    return pl.pallas_call(
        _conv_kernel,
        grid=(bsz, n // tn),
        in_specs=[pl.BlockSpec((1, t, tn), lambda b, j: (b, 0, j)),
                  pl.BlockSpec((SSD_CONV, tn), lambda b, j: (0, j)),
                  pl.BlockSpec((1, tn), lambda b, j: (0, j))],
        out_specs=pl.BlockSpec((1, t, tn), lambda b, j: (b, 0, j)),
        out_shape=jax.ShapeDtypeStruct((bsz, t, n), F32),
        compiler_params=_cparams("arbitrary", "arbitrary"),
        name="ssd_conv",
    )(xbc, w, b.reshape(1, n))


def _ssd_kernel(x_f, g_f, x_b, g_b, dtb_ref, ea_ref, yf_ref, yb_ref, s_ref):
    @pl.when(pl.program_id(1) == 0)
    def _():
        s_ref[...] = jnp.zeros_like(s_ref)

    col0 = 4 * ML_HEADS
    for d, (x_ref, g_ref, o_ref) in enumerate(((x_f, g_f, yf_ref), (x_b, g_b, yb_ref))):
        incl, _ = _tri_masks(d)
        dtv = _softplus(g_ref[0] + dtb_ref[...])
        bcum = jnp.dot(incl.astype(F32), -dtv * ea_ref[...], precision=HIGHEST, preferred_element_type=F32)
        bcum_t, dt_t = bcum.T, dtv.T
        last = CHUNK - 1 if d == 0 else 0
        for g in range(SSD_GROUPS):
            bm = x_ref[0, :, SSD_W + g * SSD_N:SSD_W + (g + 1) * SSD_N].astype(BF16)
            cm = x_ref[0, :, SSD_W + SSD_GN + g * SSD_N:SSD_W + SSD_GN + (g + 1) * SSD_N].astype(BF16)
            cb = _dot_t(cm, bm)
            s_old = s_ref[d, g]
            y_state = _dot_t(cm, s_old.astype(BF16))
            xe_parts, dec_parts = [], []
            for hh in range(SSD_HPG):
                head = g * SSD_HPG + hh
                col = col0 + d * SSD_HEADS + head
                b_col, b_row = bcum[:, col:col + 1], bcum_t[col:col + 1, :]
                dt_col, dt_row = dtv[:, col:col + 1], dt_t[col:col + 1, :]
                btot = bcum[last:last + 1, col:col + 1]
                seg = jnp.exp(jnp.where(incl, b_col - b_row, -jnp.inf))
                xh = x_ref[0, :, head * SSD_P:(head + 1) * SSD_P]
                y = _dot((cb * seg * dt_row).astype(BF16), xh.astype(BF16))
                y = y + jnp.exp(b_col) * y_state[:, hh * SSD_P:(hh + 1) * SSD_P]
                o_ref[0, :, head * SSD_P:(head + 1) * SSD_P] = y
                xe_parts.append(xh * (jnp.exp(btot - b_col) * dt_col))
                dec_parts.append(jnp.broadcast_to(jnp.exp(btot), (SSD_P, 1)))
            xe_t = jnp.concatenate(xe_parts, axis=1).T.astype(BF16)
            s_ref[d, g] = jnp.concatenate(dec_parts, axis=0) * s_old + _dot(xe_t, bm)


def _ssd(xbc, gates, dtb_row, ea_row):
    bsz, t, w = xbc.shape
    nc, nctx = t // CHUNK, CTX_LEN // CHUNK
    bwd = functools.partial(_chunk_bwd, n_ctx=nctx, n_all=nc)
    return pl.pallas_call(
        _ssd_kernel,
        grid=(bsz, nc),
        in_specs=[pl.BlockSpec((1, CHUNK, w), lambda b, i: (b, i, 0)),
                  pl.BlockSpec((1, CHUNK, LANES), lambda b, i: (b, i, 0)),
                  pl.BlockSpec((1, CHUNK, w), lambda b, i: (b, bwd(i), 0)),
                  pl.BlockSpec((1, CHUNK, LANES), lambda b, i: (b, bwd(i), 0)),
                  pl.BlockSpec((1, LANES), lambda b, i: (0, 0)),
                  pl.BlockSpec((1, LANES), lambda b, i: (0, 0))],
        out_specs=[pl.BlockSpec((1, CHUNK, SSD_W), lambda b, i: (b, i, 0)),
                   pl.BlockSpec((1, CHUNK, SSD_W), lambda b, i: (b, bwd(i), 0))],
        out_shape=[jax.ShapeDtypeStruct((bsz, t, SSD_W), F32)] * 2,
        scratch_shapes=[pltpu.VMEM((2, SSD_GROUPS, SSD_HPG * SSD_P, SSD_N), F32)],
        compiler_params=_cparams("arbitrary", "arbitrary"),
        name="ssd_scan",
    )(xbc, gates, xbc, gates, dtb_row, ea_row)


def _residual_and_next(y, h_ref, g1_ref, a2_ref, b2_ref, h_out, v_out):
    h_new = h_ref[...] + g1_ref[0] * _rms(y)
    h_out[...] = h_new
    v_out[...] = _rms(h_new) * a2_ref[0] + b2_ref[0]


def _ab_out_kernel(hf_ref, hb_ref, og_ref, yf_ref, yb_ref, xs_ref, z_ref, h_ref, mln_ref, sd_ref, sn_ref,
                   w_ref, g1_ref, a2_ref, b2_ref, h_out, v_out):
    y = None
    for hd in range(ML_HEADS):
        cols = slice(hd * ML_V, (hd + 1) * ML_V)
        ml = _rms(hf_ref[:, cols] + hb_ref[:, cols]) * mln_ref[:, cols] * jax.nn.sigmoid(og_ref[:, cols])
        part = _dot(ml.astype(BF16), w_ref[cols, :])
        y = part if y is None else y + part
    ys = (yf_ref[...] + yb_ref[...] + sd_ref[...] * xs_ref[...]) * _silu(z_ref[...])
    y = y + _dot((_rms(ys) * sn_ref[...]).astype(BF16), w_ref[ML_W:, :])
    _residual_and_next(y, h_ref, g1_ref, a2_ref, b2_ref, h_out, v_out)


def _row_spec(tm, width, col=0):
    return pl.BlockSpec((tm, width), lambda i: (i, col))


def _tab_spec():
    return pl.BlockSpec((1, 1, D_MODEL), lambda i: (i, 0, 0))


def _full_spec(shape):
    return pl.BlockSpec(shape, lambda i: tuple(0 for _ in shape))


def _ab_out(hf, hb, og, yf, yb, xbc, z, h, ml_norm, ssd_d_row, ssd_norm, w_out, g1, a2, b2):
    rows = h.shape[0]
    tm = TOK_BLK
    wide = lambda: _row_spec(tm, D_MODEL)
    return pl.pallas_call(
        _ab_out_kernel,
        grid=(rows // tm,),
        in_specs=[wide(), wide(), wide(), wide(), wide(), wide(), wide(), wide(),
                  _full_spec((1, ML_W)), _full_spec((1, SSD_W)), _full_spec((1, SSD_W)),
                  _full_spec(w_out.shape), _tab_spec(), _tab_spec(), _tab_spec()],
        out_specs=[wide(), wide()],
        out_shape=[jax.ShapeDtypeStruct((rows, D_MODEL), F32)] * 2,
        compiler_params=_cparams("arbitrary"),
        name="ab_out",
    )(hf, hb, og, yf, yb, xbc, z, h, ml_norm, ssd_d_row, ssd_norm, w_out, g1, a2, b2)


def _router_gates_t(x, rw_t, rb_col):
    tm = x.shape[0]
    aff = jax.nn.sigmoid(lax.dot_general(rw_t, x, (((1,), (1,)), ((), ())), precision=HIGHEST,
                                         preferred_element_type=F32))
    sel = aff + rb_col
    row = lambda a, e: a[e:e + 1, :]
    n = EXPERTS_PER_GROUP
    best, gi = None, None
    for g in range(N_GROUPS):
        xs = [row(sel, g * n + j) for j in range(n)]
        score = None
        for a in range(n):
            for b in range(a + 1, n):
                pair = xs[a] + xs[b]
                score = pair if score is None else jnp.maximum(score, pair)
        if g == 0:
            best, gi = score, jnp.zeros((1, tm), jnp.int32)
        else:
            upd = score > best
            gi = jnp.where(upd, g, gi)
            best = jnp.where(upd, score, best)

    def pick(a, j):
        out = row(a, j)
        for g in range(1, N_GROUPS):
            out = jnp.where(gi == g, row(a, g * n + j), out)
        return out

    sel_in = [pick(sel, j) for j in range(n)]
    aff_in = [pick(aff, j) for j in range(n)]

    def argmax_first(vals):
        bv, bi = vals[0], jnp.zeros((1, tm), jnp.int32)
        for j in range(1, n):
            upd = vals[j] > bv
            bi = jnp.where(upd, j, bi)
            bv = jnp.where(upd, vals[j], bv)
        return bi

    i1 = argmax_first(sel_in)
    i2 = argmax_first([jnp.where(i1 == j, -jnp.inf, sel_in[j]) for j in range(n)])

    def take(vals, idx):
        out = vals[0]
        for j in range(1, n):
            out = jnp.where(idx == j, vals[j], out)
        return out

    w1, w2 = take(aff_in, i1), take(aff_in, i2)
    wsum = w1 + w2
    w1, w2 = w1 / wsum, w2 / wsum
    rows = [jnp.ones((1, tm), F32)]
    for e in range(N_EXPERTS):
        g, j = divmod(e, n)
        in_g = jnp.where(i1 == j, w1, 0.0) + jnp.where(i2 == j, w2, 0.0)
        rows.append(jnp.where(gi == g, in_g, 0.0))
    rows.append(jnp.zeros((LANES - 1 - N_EXPERTS, tm), F32))
    return jnp.concatenate(rows, axis=0)


def _moe_kernel(v_ref, h_ref, rw_ref, rb_ref, wg_ref, wu_ref, wd_ref, g2_ref, o_ref, acc_ref, gate_ref, xb_ref,
                *, nsub):
    e = pl.program_id(1)

    @pl.when(e == 0)
    def _():
        x = v_ref[...]
        xb_ref[...] = x.astype(BF16)
        gate_ref[...] = _router_gates_t(x, rw_ref[...], rb_ref[...]).T
        acc_ref[...] = jnp.zeros_like(acc_ref)

    xb = xb_ref[...]
    lane = lax.broadcasted_iota(jnp.int32, gate_ref.shape, 1)
    gate = jnp.sum(jnp.where(lane == e, gate_ref[...], 0.0), axis=1, keepdims=True)
    act = _silu(_dot(xb, wg_ref[0])) * _dot(xb, wu_ref[0]) * gate
    acc_ref[...] += _dot(act.astype(BF16), wd_ref[0])

    @pl.when(e == N_EXPERTS)
    def _():
        for s in range(nsub):
            rows = slice(s * TOK_BLK, (s + 1) * TOK_BLK)
            o_ref[rows, :] = h_ref[rows, :] + g2_ref[s] * _rms(acc_ref[rows, :])


def _moe(v, h, rw_t, rb_col, wg, wu, wd, g2):
    rows = v.shape[0]
    nsub = 4
    tm = nsub * TOK_BLK
    assert rows % tm == 0
    ne = wg.shape[0]
    return pl.pallas_call(
        functools.partial(_moe_kernel, nsub=nsub),
        grid=(rows // tm, ne),
        in_specs=[pl.BlockSpec((tm, D_MODEL), lambda i, e: (i, 0)),
                  pl.BlockSpec((tm, D_MODEL), lambda i, e: (i, 0)),
                  pl.BlockSpec(rw_t.shape, lambda i, e: (0, 0)),
                  pl.BlockSpec(rb_col.shape, lambda i, e: (0, 0)),
                  pl.BlockSpec((1, D_MODEL, EXPERT_FF), lambda i, e: (e, 0, 0)),
                  pl.BlockSpec((1, D_MODEL, EXPERT_FF), lambda i, e: (e, 0, 0)),
                  pl.BlockSpec((1, EXPERT_FF, D_MODEL), lambda i, e: (e, 0, 0)),
                  pl.BlockSpec((nsub, 1, D_MODEL), lambda i, e: (i, 0, 0))],
        out_specs=pl.BlockSpec((tm, D_MODEL), lambda i, e: (i, 0)),
        out_shape=jax.ShapeDtypeStruct((rows, D_MODEL), F32),
        scratch_shapes=[pltpu.VMEM((tm, D_MODEL), F32), pltpu.VMEM((tm, LANES), F32),
                        pltpu.VMEM((tm, D_MODEL), BF16)],
        compiler_params=_cparams("arbitrary", "arbitrary"),
        name="moe",
    )(v, h, rw_t, rb_col, wg, wu, wd, g2)


def _mla_prep_kernel(m_ref, qn_ref, kvn_ref, wq_ref, wkv_ref, cq_ref, sq_ref, pq_ref, ck_ref, sk_ref, pk_ref,
                     q_ref, kn_ref, v_ref, kr_ref):
    n_nope = MLA_HEADS * MLA_NOPE
    ql = _rms(m_ref[:, :MLA_Q_RANK]) * qn_ref[...]
    q = _dot(ql.astype(BF16), wq_ref[...])
    qr = q[:, n_nope:]
    qr = qr * cq_ref[...] + jnp.dot(qr, pq_ref[...], precision=HIGHEST, preferred_element_type=F32) * sq_ref[...]
    q_ref[:, :n_nope] = (q[:, :n_nope] * MLA_SCALE).astype(q_ref.dtype)
    q_ref[:, n_nope:] = (qr * MLA_SCALE).astype(q_ref.dtype)
    kvl = _rms(m_ref[:, MLA_Q_RANK:MLA_Q_RANK + MLA_KV_RANK]) * kvn_ref[...]
    kv = _dot(kvl.astype(BF16), wkv_ref[...])
    kn_ref[...] = kv[:, :n_nope].astype(kn_ref.dtype)
    v_ref[...] = kv[:, n_nope:].astype(v_ref.dtype)
    kr = m_ref[:, MLA_Q_RANK + MLA_KV_RANK:MLA_Q_RANK + MLA_KV_RANK + MLA_ROPE]
    kr = kr * ck_ref[...] + jnp.dot(kr, pk_ref[...], precision=HIGHEST, preferred_element_type=F32) * sk_ref[...]
    kr_ref[...] = kr.astype(kr_ref.dtype)


def _rope_tables(t_len, width):
    n_lat = t_len - CTX_LEN
    pos = jnp.arange(n_lat)
    n_freq = MLA_ROPE // 4
    inv = ROPE_BASE ** (-jnp.arange(n_freq, dtype=F32) / n_freq)
    ang = jnp.concatenate([(pos // GRID_W).astype(F32)[:, None] * inv, (pos % GRID_W).astype(F32)[:, None] * inv],
                          axis=-1)
    cos = jnp.concatenate([jnp.ones((CTX_LEN, MLA_ROPE // 2), F32), jnp.cos(ang)], axis=0)
    sin = jnp.concatenate([jnp.zeros((CTX_LEN, MLA_ROPE // 2), F32), jnp.sin(ang)], axis=0)
    reps = width // MLA_ROPE
    cos_t = jnp.tile(jnp.concatenate([cos, cos], axis=1), (1, reps))
    sin_t = jnp.tile(jnp.concatenate([-sin, sin], axis=1), (1, reps))
    idx = jnp.arange(width)
    swap = (idx[:, None] == (idx[None, :] ^ (MLA_ROPE // 2))).astype(F32)
    return cos_t, sin_t, swap


def _mla_prep(m, qn, kvn, wq, wkv, bsz):
    rows, width = m.shape
    t = rows // bsz
    tm = TOK_BLK
    nb = t // tm
    n_nope, n_rope = MLA_HEADS * MLA_NOPE, MLA_HEADS * MLA_ROPE
    cq, sq, pq = _rope_tables(t, n_rope)
    ck, sk, pk = _rope_tables(t, MLA_ROPE)
    pos = lambda w: pl.BlockSpec((tm, w), lambda i: (i % nb, 0))
    return pl.pallas_call(
        _mla_prep_kernel,
        grid=(rows // tm,),
        in_specs=[_row_spec(tm, width), _full_spec(qn.shape), _full_spec(kvn.shape), _full_spec(wq.shape),
                  _full_spec(wkv.shape), pos(n_rope), pos(n_rope), _full_spec(pq.shape), pos(MLA_ROPE),
                  pos(MLA_ROPE), _full_spec(pk.shape)],
        out_specs=[_row_spec(tm, n_nope + n_rope), _row_spec(tm, n_nope), _row_spec(tm, MLA_HEADS * MLA_V),
                   _row_spec(tm, MLA_ROPE)],
        out_shape=[jax.ShapeDtypeStruct((rows, n_nope + n_rope), BF16),
                   jax.ShapeDtypeStruct((rows, n_nope), BF16),
                   jax.ShapeDtypeStruct((rows, MLA_HEADS * MLA_V), BF16),
                   jax.ShapeDtypeStruct((rows, MLA_ROPE), BF16)],
        compiler_params=_cparams("arbitrary"),
        name="mla_prep",
    )(m, qn, kvn, wq, wkv, cq, sq, pq, ck, sk, pk)


def _attn_kernel(q_ref, kn_ref, kr_ref, v_ref, o_ref):
    n_nope = MLA_HEADS * MLA_NOPE
    kr = kr_ref[0]
    for hd in range(MLA_HEADS):
        qn = q_ref[:, hd * MLA_NOPE:(hd + 1) * MLA_NOPE]
        qr = q_ref[:, n_nope + hd * MLA_ROPE:n_nope + (hd + 1) * MLA_ROPE]
        s = _dot_t(qn, kn_ref[0, :, hd * MLA_NOPE:(hd + 1) * MLA_NOPE]) + _dot_t(qr, kr)
        p = jnp.exp(s - jnp.max(s, axis=-1, keepdims=True))
        o = _dot(p.astype(BF16), v_ref[0, :, hd * MLA_V:(hd + 1) * MLA_V])
        o_ref[:, hd * MLA_V:(hd + 1) * MLA_V] = (o / jnp.sum(p, axis=-1, keepdims=True)).astype(o_ref.dtype)


def _lat_block(i, nb):
    return (i // (nb - 1)) * nb + 1 + i % (nb - 1)


def _attention(q, kn, kr, v, bsz):
    t = kn.shape[1]
    tq = TOK_BLK
    nb = t // tq
    n_lat = bsz * (nb - 1)
    kv_spec = lambda w: pl.BlockSpec((1, t, w), lambda i: (i // (nb - 1), 0, 0))
    return pl.pallas_call(
        _attn_kernel,
        grid=(n_lat,),
        in_specs=[pl.BlockSpec((tq, q.shape[1]), lambda i: (_lat_block(i, nb), 0)),
                  kv_spec(kn.shape[2]), kv_spec(kr.shape[2]), kv_spec(v.shape[2])],
        out_specs=_row_spec(tq, MLA_HEADS * MLA_V),
        out_shape=jax.ShapeDtypeStruct((n_lat * tq, MLA_HEADS * MLA_V), BF16),
        compiler_params=_cparams("arbitrary"),
        name="mla_attention",
    )(q, kn, kr, v)


def _seg_sum(x):
    lane = lax.broadcasted_iota(jnp.int32, (1, 2 * RW_N), 1)
    s0 = jnp.sum(x[:, :RW_N], axis=1, keepdims=True)
    s1 = jnp.sum(x[:, RW_N:], axis=1, keepdims=True)
    return jnp.where(lane < RW_N, s0, s1)


def _rwkv_chunk(d, r, v, a, kd, bb, lw, s_ref):
    incl, strict = _tri_masks(d)
    last = CHUNK - 1 if d == 0 else 0
    cl = jnp.dot(incl.astype(F32), lw, precision=HIGHEST, preferred_element_type=F32)
    cl_tot = cl[last:last + 1, :]
    p_inv = jnp.exp(-cl)
    p_end = jnp.exp(cl_tot - cl)
    at, rt = a * jnp.exp(cl - lw), r * jnp.exp(cl)
    bt, kt = bb * p_inv, kd * p_inv
    bh, kh = bb * p_end, kd * p_end
    dec = jnp.exp(cl_tot)
    outs = []
    for hh in range(2):
        cols = slice(hh * RW_N, (hh + 1) * RW_N)
        ar = jnp.concatenate([at[:, cols], rt[:, cols]], axis=0).astype(BF16)
        bk = jnp.concatenate([bt[:, cols], kt[:, cols]], axis=0).astype(BF16)
        vh = v[:, cols]
        s_old = s_ref[2 * d + hh]
        m4 = _dot_t(ar, bk)
        arh = _dot_t(ar, s_old.astype(BF16))
        nmat = jnp.where(strict, m4[:CHUNK, :CHUNK], 0.0)
        u = arh[:CHUNK] + _dot(jnp.where(strict, m4[:CHUNK, CHUNK:], 0.0).astype(BF16), vh.astype(BF16))
        step = CHUNK // 2
        pw = nmat
        while True:
            u = u + _dot(pw.astype(BF16), u.astype(BF16))
            step //= 2
            if step == 0:
                break
            pw = _dot(pw.astype(BF16), pw.astype(BF16))
        uv = jnp.concatenate([u, vh], axis=0)
        rbk = jnp.concatenate([jnp.where(incl, m4[CHUNK:, :CHUNK], 0.0), jnp.where(incl, m4[CHUNK:, CHUNK:], 0.0)],
                              axis=1)
        outs.append(arh[CHUNK:] + _dot(rbk.astype(BF16), uv.astype(BF16)))
        bkh = jnp.concatenate([bh[:, cols], kh[:, cols]], axis=0).astype(BF16)
        s_ref[2 * d + hh] = dec[:, cols] * s_old + _dot(uv.T.astype(BF16), bkh)
    return jnp.concatenate(outs, axis=1)


def _rwkv_kernel(r_ref, k_ref, v_ref, low_ref, mur_ref, muk_ref, muv_ref, mul_ref, w0_ref, w2_ref, a0_ref,
                 a2_ref, g2_ref, kk_ref, ka_ref, rk_ref, ln_ref, o_ref,
                 r_s, v_s, a_s, bon_s, gate_s, o_s, lw_s, kd_s, bb_s, s_ref, *, n_chunks, n_ctx):
    t_len = r_s.shape[0]
    t = lax.broadcasted_iota(jnp.int32, (t_len, 1), 0)
    lo = jnp.where(t < CTX_LEN, 0, CTX_LEN)
    hi = jnp.where(t < CTX_LEN, CTX_LEN, t_len)

    def shift_mix(x, mu):
        prev = jnp.where(t - 1 >= lo, pltpu.roll(x, shift=1, axis=0), 0.0)
        nxt = jnp.where(t + 1 < hi, pltpu.roll(x, shift=t_len - 1, axis=0), 0.0)
        return x + mu * (0.5 * (prev + nxt) - x)

    r = shift_mix(r_ref[0], mur_ref[...])
    k = shift_mix(k_ref[0], muk_ref[...])
    v = shift_mix(v_ref[0], muv_ref[...])
    low = shift_mix(low_ref[0], mul_ref[...])
    kk = k * kk_ref[...]
    kk = kk / jnp.maximum(jnp.sqrt(_seg_sum(kk * kk)), 1e-12)
    r_s[...] = r
    v_s[...] = v
    a_s[...] = -kk
    bonus = jnp.zeros_like(r)
    for d in range(2):
        wl = jnp.tanh(low[:, d * RW_DECAY_RANK:(d + 1) * RW_DECAY_RANK])
        al = low[:, 2 * RW_DECAY_RANK + d * RW_A_RANK:2 * RW_DECAY_RANK + (d + 1) * RW_A_RANK]
        wz = w0_ref[d:d + 1, :] + jnp.dot(wl, w2_ref[d], precision=HIGHEST, preferred_element_type=F32)
        lw_s[d] = -jnp.exp(-_softplus(-wz) - 0.5)
        a = jax.nn.sigmoid(a0_ref[d:d + 1, :] + jnp.dot(al, a2_ref[d], precision=HIGHEST,
                                                        preferred_element_type=F32))
        kd = k * (1.0 + (a - 1.0) * ka_ref[...])
        kd_s[d] = kd
        bb_s[d] = kk * a
        bonus = bonus + _seg_sum(r * kd * rk_ref[...])
    bon_s[...] = bonus * v
    gl = jax.nn.sigmoid(low[:, 2 * RW_DECAY_RANK + 2 * RW_A_RANK:])
    gate_s[...] = jnp.dot(gl, g2_ref[...], precision=HIGHEST, preferred_element_type=F32)
    o_s[...] = jnp.zeros_like(o_s)
    s_ref[...] = jnp.zeros_like(s_ref)

    def body(i, carry):
        for d in range(2):
            c = i if d == 0 else _chunk_bwd(i, n_ctx, n_chunks)
            rows = pl.ds(pl.multiple_of(c * CHUNK, CHUNK), CHUNK)
            y = _rwkv_chunk(d, r_s[rows, :], v_s[rows, :], a_s[rows, :], kd_s[d, rows, :], bb_s[d, rows, :],
                            lw_s[d, rows, :], s_ref)
            o_s[rows, :] += y
        return carry

    lax.fori_loop(0, n_chunks, body, 0)
    o = o_s[...]
    y = o * lax.rsqrt(_seg_sum(o * o) * (1.0 / RW_N) + EPS) * ln_ref[...] + bon_s[...]
    o_ref[0] = y * gate_s[...]


def _rwkv(rkv, low, mu, w0, w2, a0, a2, g2, kk, ka, rk, ln):
    bsz, t, _ = rkv.shape
    hp = 2 * RW_N
    n_hp = RW_W // hp
    n_low = low.shape[2]
    tok = lambda off: pl.BlockSpec((1, t, hp), lambda b, j: (b, 0, off * n_hp + j))
    vec = lambda off: pl.BlockSpec((1, hp), lambda b, j: (0, off * n_hp + j))
    par2 = pl.BlockSpec((2, hp), lambda b, j: (0, j))
    par3 = pl.BlockSpec((2, RW_DECAY_RANK, hp), lambda b, j: (0, 0, j))
    tbuf = lambda: pltpu.VMEM((t, hp), F32)
    dbuf = lambda: pltpu.VMEM((2, t, hp), F32)
    mu_rkv, mu_low = mu[None, :3 * RW_W], mu[None, 3 * RW_W:]
    row = lambda a: a.reshape(1, RW_W)
    return pl.pallas_call(
        functools.partial(_rwkv_kernel, n_chunks=t // CHUNK, n_ctx=CTX_LEN // CHUNK),
        grid=(bsz, n_hp),
        in_specs=[tok(0), tok(1), tok(2), pl.BlockSpec((1, t, n_low), lambda b, j: (b, 0, 0)),
                  vec(0), vec(1), vec(2), pl.BlockSpec((1, n_low), lambda b, j: (0, 0)),
                  par2, par3, par2, par3, pl.BlockSpec((RW_GATE_RANK, hp), lambda b, j: (0, j)),
                  vec(0), vec(0), vec(0), vec(0)],
        out_specs=pl.BlockSpec((1, t, hp), lambda b, j: (b, 0, j)),
        out_shape=jax.ShapeDtypeStruct((bsz, t, RW_W), F32),
        scratch_shapes=[tbuf(), tbuf(), tbuf(), tbuf(), tbuf(), tbuf(), dbuf(), dbuf(), dbuf(),
                        pltpu.VMEM((4, RW_N, RW_N), F32)],
        compiler_params=_cparams("arbitrary", "arbitrary"),
        name="rwkv7",
    )(rkv, rkv, rkv, low, mu_rkv, mu_rkv, mu_rkv, mu_low, w0, w2, a0, a2, g2, row(kk), row(ka), row(rk), row(ln))


def _cd_out_kernel(att_ref, rw_ref, h_ref, w_ref, g1_ref, a2_ref, b2_ref, h_out, v_out):
    n_att = MLA_HEADS * MLA_V
    y = _dot(att_ref[...], w_ref[:n_att, :]) + _dot(rw_ref[...].astype(BF16), w_ref[n_att:, :])
    _residual_and_next(y, h_ref, g1_ref, a2_ref, b2_ref, h_out, v_out)


def _cd_out(att, rw, h, w_out, g1, a2, b2, nb):
    rows = att.shape[0]
    tm = TOK_BLK
    lat = lambda: pl.BlockSpec((tm, D_MODEL), lambda i: (_lat_block(i, nb), 0))
    return pl.pallas_call(
        _cd_out_kernel,
        grid=(rows // tm,),
        in_specs=[_row_spec(tm, att.shape[1]), lat(), lat(), _full_spec(w_out.shape), _tab_spec(), _tab_spec(),
                  _tab_spec()],
        out_specs=[_row_spec(tm, D_MODEL), _row_spec(tm, D_MODEL)],
        out_shape=[jax.ShapeDtypeStruct((rows, D_MODEL), F32)] * 2,
        compiler_params=_cparams("arbitrary"),
        name="cd_out",
    )(att, rw, h, w_out, g1, a2, b2)


def _pad_cols(w, width):
    return jnp.pad(w, ((0, 0), (0, width - w.shape[1])))


def _split_cols(w, sizes):
    assert sum(sizes) == w.shape[1]
    out, start = [], 0
    for s in sizes:
        out.append(w[:, start:start + s])
        start += s
    return out


def _layer_ab(h, tabs, ab_w_in, ab_w_out, ml_i_bias, ml_f_bias, ml_norm, ssd_conv_w, ssd_conv_b, ssd_dt_bias,
              ssd_a_log, ssd_d, ssd_norm, bsz):
    a1, b1, g1, a2, b2, _ = tabs
    t = h.shape[0] // bsz
    q, k, v, og, ig, fg, z, xbc, dt = _split_cols(
        ab_w_in, (ML_QKW, ML_QKW, ML_W, ML_W, 2 * ML_HEADS, 2 * ML_HEADS, SSD_W, SSD_XBC, 2 * SSD_HEADS))
    small = _pad_cols(jnp.concatenate([ig, fg, dt], axis=1), LANES)
    w_in = jnp.concatenate([q, k, v, og, z, xbc, small], axis=1).astype(BF16)
    widths = (2 * ML_QKW + ML_W, ML_W, SSD_W, SSD_XBC, LANES)
    starts = [0]
    for wd in widths[:-1]:
        starts.append(starts[-1] + wd)
    qkv, og_p, z_p, xbc_p, small_p = _norm_proj(h, a1, b1, w_in, tuple(zip(starts, widths)),
                                               (BF16, F32, F32, F32, F32), "ab_in_proj")
    pad = LANES - 4 * ML_HEADS
    bias_row = jnp.concatenate([ml_i_bias.reshape(-1), ml_f_bias.reshape(-1), jnp.zeros((pad,), F32)])[None]
    hf, hb = _mlstm(qkv.reshape(bsz, t, -1), small_p.reshape(bsz, t, LANES), bias_row)
    xbc_c = _ssd_conv(xbc_p.reshape(bsz, t, SSD_XBC), ssd_conv_w, ssd_conv_b)
    lead = jnp.zeros((4 * ML_HEADS,), F32)
    tail = jnp.zeros((LANES - 4 * ML_HEADS - 2 * SSD_HEADS,), F32)
    dtb_row = jnp.concatenate([lead, ssd_dt_bias.reshape(-1), tail])[None]
    ea_row = jnp.concatenate([lead, jnp.exp(ssd_a_log).reshape(-1), tail])[None]
    yf, yb = _ssd(xbc_c, small_p.reshape(bsz, t, LANES), dtb_row, ea_row)
    flat = lambda a: a.reshape(bsz * t, -1)
    return _ab_out(flat(hf), flat(hb), og_p, flat(yf), flat(yb), flat(xbc_c), z_p, h, ml_norm[None],
                   jnp.repeat(ssd_d, SSD_P)[None], ssd_norm[None], ab_w_out.astype(BF16), g1, a2, b2)


def _heads_to_front(w, n_heads, first):
    k, n = w.shape
    w3 = w.reshape(k, n_heads, n // n_heads)
    return jnp.concatenate([w3[:, :, :first].reshape(k, -1), w3[:, :, first:].reshape(k, -1)], axis=1)


def _layer_cd(h, tabs_all, tabs_lat, cd_w_in, cd_w_out, mla_qn, mla_w_uq, mla_kvn, mla_w_ukv, rw_mu, rw_w0, rw_w2,
              rw_a0, rw_a2, rw_g2, rw_kk, rw_ka, rw_rk, rw_ln, bsz):
    a1, b1 = tabs_all[0], tabs_all[1]
    _, _, g1, a2, b2, _ = tabs_lat
    t = h.shape[0] // bsz
    n_mla = MLA_Q_RANK + MLA_KV_RANK + MLA_ROPE
    n_low = 2 * RW_DECAY_RANK + 2 * RW_A_RANK + RW_GATE_RANK
    mla_w = 6 * LANES
    w_in = jnp.concatenate([_pad_cols(cd_w_in[:, :n_mla], mla_w), cd_w_in[:, n_mla:]], axis=1).astype(BF16)
    segs = ((0, mla_w), (mla_w, 3 * RW_W), (mla_w + 3 * RW_W, n_low))
    m, rkv, low = _norm_proj(h, a1, b1, w_in, segs, (F32, F32, F32), "cd_in_proj")
    wq = _heads_to_front(mla_w_uq, MLA_HEADS, MLA_NOPE).astype(BF16)
    wkv = _heads_to_front(mla_w_ukv, MLA_HEADS, MLA_NOPE).astype(BF16)
    q, kn, vv, kr = _mla_prep(m, mla_qn[None], mla_kvn[None], wq, wkv, bsz)
    per_b = lambda a: a.reshape(bsz, t, -1)
    att = _attention(q, per_b(kn), per_b(kr), per_b(vv), bsz)
    rw = _rwkv(per_b(rkv), per_b(low), rw_mu, rw_w0, rw_w2, rw_a0, rw_a2, rw_g2, rw_kk, rw_ka, rw_rk, rw_ln)
    return _cd_out(att, rw.reshape(bsz * t, RW_W), h, cd_w_out.astype(BF16), g1, a2, b2, t // TOK_BLK)


def _moe_weights(router_w, router_bias, exp_w_gate, exp_w_up, exp_w_down, sh_w_gate, sh_w_up, sh_w_down):
    stack = lambda s, e: jnp.concatenate([s[None], e], axis=0).astype(BF16)
    return (router_w.T, router_bias[:, None], stack(sh_w_gate, exp_w_gate), stack(sh_w_up, exp_w_up),
            stack(sh_w_down, exp_w_down))


def kernel(x, c, ctx, c_ctx, w_mod, b_mod, norm_g, ab_w_in, ab_w_out, ml_i_bias, ml_f_bias, ml_norm, ssd_conv_w, ssd_conv_b, ssd_dt_bias, ssd_a_log, ssd_d, ssd_norm, cd_w_in, cd_w_out, mla_qn, mla_w_uq, mla_kvn, mla_w_ukv, rw_mu, rw_w0, rw_w2, rw_a0, rw_a2, rw_g2, rw_kk, rw_ka, rw_rk, rw_ln, router_w, router_bias, exp_w_gate, exp_w_up, exp_w_down, sh_w_gate, sh_w_up, sh_w_down):
    bsz, seq, _ = x.shape
    t = CTX_LEN + seq
    h = jnp.concatenate([ctx, x], axis=1).reshape(bsz * t, D_MODEL)

    mod = _modulation(c, c_ctx, w_mod[0], b_mod[0])
    tabs = _block_tables(mod, norm_g[0], bsz, t // TOK_BLK, True)
    h, v = _layer_ab(h, tabs, ab_w_in[0], ab_w_out[0], ml_i_bias[0], ml_f_bias[0], ml_norm[0], ssd_conv_w[0],
                     ssd_conv_b[0], ssd_dt_bias[0], ssd_a_log[0], ssd_d[0], ssd_norm[0], bsz)
    moe_w = _moe_weights(router_w, router_bias, exp_w_gate[0], exp_w_up[0], exp_w_down[0], sh_w_gate[0],
                         sh_w_up[0], sh_w_down[0])
    h = _moe(v, h, *moe_w, tabs[5])

    mod = _modulation(c, c_ctx, w_mod[1], b_mod[1])
    nb = t // TOK_BLK
    tabs_all = _block_tables(mod, norm_g[1], bsz, nb, True)
    tabs_lat = _block_tables(mod, norm_g[1], bsz, nb - 1, False)
    h, v = _layer_cd(h, tabs_all, tabs_lat, cd_w_in[0], cd_w_out[0], mla_qn[0], mla_w_uq[0], mla_kvn[0],
                     mla_w_ukv[0], rw_mu[0], rw_w0[0], rw_w2[0], rw_a0[0], rw_a2[0], rw_g2[0], rw_kk[0], rw_ka[0],
                     rw_rk[0], rw_ln[0], bsz)
    moe_w = _moe_weights(router_w, router_bias, exp_w_gate[1], exp_w_up[1], exp_w_down[1], sh_w_gate[1],
                         sh_w_up[1], sh_w_down[1])
    h = _moe(v, h, *moe_w, tabs_lat[5])
    return h.reshape(bsz, seq, D_MODEL)
```

```python
import functools

import jax
import jax.numpy as jnp
from jax import lax
from jax.experimental import pallas as pl
from jax.experimental.pallas import tpu as pltpu

F32 = jnp.float32
BF16 = jnp.bfloat16
HIGHEST = lax.Precision.HIGHEST

D_MODEL = 1024
CTX_LEN = 256
GRID_W = 64
EPS = 1e-6
ROPE_BASE = 10000.0
TOK_BLK = 256
CHUNK = 64
LANES = 128

ML_HEADS, ML_QK, ML_V = 4, 128, 256
ML_QKW, ML_W = ML_HEADS * ML_QK, ML_HEADS * ML_V
SSD_HEADS, SSD_P, SSD_N, SSD_GROUPS, SSD_CONV = 16, 64, 64, 2, 5
SSD_HPG = SSD_HEADS // SSD_GROUPS
SSD_W, SSD_GN = SSD_HEADS * SSD_P, SSD_GROUPS * SSD_N
SSD_XBC = SSD_W + 2 * SSD_GN
MLA_HEADS, MLA_NOPE, MLA_ROPE, MLA_V = 8, 64, 32, 64
MLA_Q_RANK, MLA_KV_RANK = 384, 256
MLA_SCALE = (MLA_NOPE + MLA_ROPE) ** -0.5
RW_HEADS, RW_N = 16, 64
RW_W = RW_HEADS * RW_N
RW_DECAY_RANK, RW_A_RANK, RW_GATE_RANK = 64, 64, 128
N_EXPERTS, N_GROUPS, EXPERT_FF = 16, 4, 512
EXPERTS_PER_GROUP = N_EXPERTS // N_GROUPS

VMEM_LIMIT = 56 * 1024 * 1024


def _cparams(*sem):
    return pltpu.CompilerParams(dimension_semantics=sem, vmem_limit_bytes=VMEM_LIMIT)


def _silu(x):
    return x * jax.nn.sigmoid(x)


def _softplus(x):
    return jnp.maximum(x, 0.0) + jnp.log1p(jnp.exp(-jnp.abs(x)))


def _log_sigmoid(x):
    return jnp.minimum(x, 0.0) - jnp.log1p(jnp.exp(-jnp.abs(x)))


def _rms(x):
    return x * lax.rsqrt(jnp.mean(x * x, axis=-1, keepdims=True) + EPS)


def _dot_t(a, b):
    return lax.dot_general(a, b, (((1,), (1,)), ((), ())), preferred_element_type=F32)


def _dot(a, b):
    return jnp.dot(a, b, preferred_element_type=F32)


def _split3(x):
    hi = x.astype(BF16)
    r1 = x - hi.astype(F32)
    mid = r1.astype(BF16)
    return hi, mid, (r1 - mid.astype(F32)).astype(BF16)


def _dot_01(m, x):
    hi, mid, lo = _split3(x)
    return _dot(m, hi) + _dot(m, mid) + _dot(m, lo)


def _dot_x01(x, m):
    hi, mid, lo = _split3(x)
    return _dot(hi, m) + _dot(mid, m) + _dot(lo, m)


def _tri_masks(d):
    row = lax.broadcasted_iota(jnp.int32, (CHUNK, CHUNK), 0)
    col = lax.broadcasted_iota(jnp.int32, (CHUNK, CHUNK), 1)
    incl = (col <= row) if d == 0 else (col >= row)
    strict = (col < row) if d == 0 else (col > row)
    return incl, strict


def _mod_kernel(s_ref, w_ref, b_ref, o_ref):
    s = _silu(s_ref[...])
    o_ref[...] = jnp.dot(s, w_ref[...], precision=HIGHEST, preferred_element_type=F32) + b_ref[...]


def _modulation(c, c_ctx, w, b):
    bsz = c.shape[0]
    rows = 8 * ((bsz + 1 + 7) // 8)
    s = jnp.zeros((rows, D_MODEL), F32).at[:bsz].set(c).at[bsz].set(c_ctx)
    n = w.shape[1]
    return pl.pallas_call(
        _mod_kernel,
        grid=(n // D_MODEL,),
        in_specs=[pl.BlockSpec((rows, D_MODEL), lambda j: (0, 0)),
                  pl.BlockSpec((D_MODEL, D_MODEL), lambda j: (0, j)),
                  pl.BlockSpec((1, D_MODEL), lambda j: (0, j))],
        out_specs=pl.BlockSpec((rows, D_MODEL), lambda j: (0, j)),
        out_shape=jax.ShapeDtypeStruct((rows, n), F32),
        compiler_params=_cparams("arbitrary"),
        name="modulation",
    )(s, w, b.reshape(1, n))


def _block_tables(mod, norm_g, bsz, blocks_per_row, with_ctx):
    j = jnp.arange(blocks_per_row)
    b = jnp.arange(bsz)
    if with_ctx:
        sel = jnp.where(j[None, :] == 0, bsz, b[:, None]).reshape(-1)
    else:
        sel = jnp.broadcast_to(b[:, None], (bsz, blocks_per_row)).reshape(-1)
    m = mod[sel]
    sh_a, sc_a, g_a, sh_f, sc_f, g_f = jnp.split(m, 6, axis=-1)
    tabs = (norm_g[0] * (1 + sc_a), sh_a, g_a * norm_g[1], norm_g[2] * (1 + sc_f), sh_f, g_f * norm_g[3])
    return tuple(t[:, None, :] for t in tabs)


def _norm_proj_kernel(x_ref, a_ref, b_ref, w_ref, *refs, segs, nsub):
    out_refs, u_ref = refs[:len(segs)], refs[len(segs)]
    for s in range(nsub):
        rows = slice(s * TOK_BLK, (s + 1) * TOK_BLK)
        u_ref[rows, :] = (_rms(x_ref[rows, :]) * a_ref[s] + b_ref[s]).astype(BF16)
    u = u_ref[...]
    for o_ref, (start, width) in zip(out_refs, segs):
        o_ref[...] = _dot(u, w_ref[:, start:start + width]).astype(o_ref.dtype)


def _norm_proj(x, a_tab, b_tab, w, segs, dtypes, name):
    rows = x.shape[0]
    nsub = 2
    tm = nsub * TOK_BLK
    assert rows % tm == 0
    n = w.shape[1]
    return pl.pallas_call(
        functools.partial(_norm_proj_kernel, segs=segs, nsub=nsub),
        grid=(rows // tm,),
        in_specs=[pl.BlockSpec((tm, D_MODEL), lambda i: (i, 0)),
                  pl.BlockSpec((nsub, 1, D_MODEL), lambda i: (i, 0, 0)),
                  pl.BlockSpec((nsub, 1, D_MODEL), lambda i: (i, 0, 0)),
                  pl.BlockSpec((D_MODEL, n), lambda i: (0, 0))],
        out_specs=[pl.BlockSpec((tm, wd), lambda i: (i, 0)) for _, wd in segs],
        out_shape=[jax.ShapeDtypeStruct((rows, wd), dt) for (_, wd), dt in zip(segs, dtypes)],
        scratch_shapes=[pltpu.VMEM((tm, D_MODEL), BF16)],
        compiler_params=_cparams("arbitrary"),
        name=name,
    )(x, a_tab, b_tab, w)


def _chunk_fwd(i):
    return i


def _chunk_bwd(i, n_ctx, n_all):
    return jnp.where(i < n_ctx, n_ctx - 1 - i, n_all + n_ctx - 1 - i)


def _mlstm_kernel(qkv_f, g_f, qkv_b, g_b, bias_ref, hf_ref, hb_ref, c_ref, n_ref, m_ref):
    @pl.when(pl.program_id(1) == 0)
    def _():
        c_ref[...] = jnp.zeros_like(c_ref)
        n_ref[...] = jnp.zeros_like(n_ref)
        m_ref[...] = jnp.zeros_like(m_ref)

    scale = ML_QK ** -0.5
    for d, (qkv_ref, g_ref, o_ref) in enumerate(((qkv_f, g_f, hf_ref), (qkv_b, g_b, hb_ref))):
        incl, _ = _tri_masks(d)
        pre = g_ref[0] + bias_ref[...]
        bcum = _dot_01(incl.astype(BF16), _log_sigmoid(pre))
        pre_t, bcum_t = pre.T, bcum.T
        last = CHUNK - 1 if d == 0 else 0
        for h in range(ML_HEADS):
            ci = d * ML_HEADS + h
            cf = 2 * ML_HEADS + ci
            b_col, b_row = bcum[:, cf:cf + 1], bcum_t[cf:cf + 1, :]
            ig_col, ig_row = pre[:, ci:ci + 1], pre_t[ci:ci + 1, :]
            btot = bcum[last:last + 1, cf:cf + 1]
            m_old = m_ref[ci:ci + 1, 0:1]
            log_d = jnp.where(incl, b_col - b_row + ig_row, -jnp.inf)
            inter = b_col + m_old
            m_j = jnp.maximum(inter, jnp.max(log_d, axis=1, keepdims=True))
            q = qkv_ref[0, :, h * ML_QK:(h + 1) * ML_QK]
            k = qkv_ref[0, :, ML_QKW + h * ML_QK:ML_QKW + (h + 1) * ML_QK]
            v = qkv_ref[0, :, 2 * ML_QKW + h * ML_V:2 * ML_QKW + (h + 1) * ML_V]
            s = _dot_t(q, k) * scale * jnp.exp(log_d - m_j)
            w_int = jnp.exp(inter - m_j)
            c_old = c_ref[ci]
            n_old = n_ref[ci]
            cq = _dot(q, c_old.astype(BF16)) * scale
            nq = jnp.sum(q.astype(F32) * n_old, axis=1, keepdims=True) * scale
            num = _dot(s.astype(BF16), v) + w_int * cq
            den = jnp.sum(s, axis=1, keepdims=True) + w_int * nq
            o_ref[0, :, h * ML_V:(h + 1) * ML_V] = num / jnp.maximum(jnp.abs(den), jnp.exp(-m_j))
            log_e_row = btot - b_row + ig_row
            log_e_col = btot - b_col + ig_col
            m_new = jnp.maximum(btot + m_old, jnp.max(log_e_row, axis=1, keepdims=True))
            e_col = jnp.exp(log_e_col - m_new)
            dec = jnp.exp(btot + m_old - m_new)
            kf = k.astype(F32)
            c_ref[ci] = dec * c_old + _dot(kf.T.astype(BF16), (v.astype(F32) * e_col).astype(BF16))
            n_ref[ci] = dec * n_old + jnp.sum(e_col * kf, axis=0, keepdims=True)
            m_ref[ci:ci + 1, :] = jnp.broadcast_to(m_new, (1, LANES))


def _mlstm(qkv, gates, bias_row):
    bsz, t, _ = qkv.shape
    nc, nctx = t // CHUNK, CTX_LEN // CHUNK
    bwd = functools.partial(_chunk_bwd, n_ctx=nctx, n_all=nc)
    w = qkv.shape[2]
    return pl.pallas_call(
        _mlstm_kernel,
        grid=(bsz, nc),
        in_specs=[pl.BlockSpec((1, CHUNK, w), lambda b, i: (b, i, 0)),
                  pl.BlockSpec((1, CHUNK, LANES), lambda b, i: (b, i, 0)),
                  pl.BlockSpec((1, CHUNK, w), lambda b, i: (b, bwd(i), 0)),
                  pl.BlockSpec((1, CHUNK, LANES), lambda b, i: (b, bwd(i), 0)),
                  pl.BlockSpec((1, LANES), lambda b, i: (0, 0))],
        out_specs=[pl.BlockSpec((1, CHUNK, ML_W), lambda b, i: (b, i, 0)),
                   pl.BlockSpec((1, CHUNK, ML_W), lambda b, i: (b, bwd(i), 0))],
        out_shape=[jax.ShapeDtypeStruct((bsz, t, ML_W), F32)] * 2,
        scratch_shapes=[pltpu.VMEM((2 * ML_HEADS, ML_QK, ML_V), F32),
                        pltpu.VMEM((2 * ML_HEADS, 1, ML_QK), F32),
                        pltpu.VMEM((2 * ML_HEADS, LANES), F32)],
        compiler_params=_cparams("arbitrary", "arbitrary"),
        name="mlstm_scan",
    )(qkv, gates, qkv, gates, bias_row)


def _conv_kernel(x_ref, w_ref, b_ref, o_ref):
    x = x_ref[0]
    t_len = x.shape[0]
    t = lax.broadcasted_iota(jnp.int32, (t_len, 1), 0)
    lo = jnp.where(t < CTX_LEN, 0, CTX_LEN)
    hi = jnp.where(t < CTX_LEN, CTX_LEN, t_len)
    half = SSD_CONV // 2
    acc = x * w_ref[half:half + 1, :]
    for tap in range(SSD_CONV):
        d = tap - half
        if d == 0:
            continue
        shifted = pltpu.roll(x, shift=(-d) % t_len, axis=0)
        ok = (t + d >= lo) & (t + d < hi)
        acc = acc + jnp.where(ok, shifted, 0.0) * w_ref[tap:tap + 1, :]
    o_ref[0] = _silu(acc + b_ref[...])


def _ssd_conv(xbc, w, b):
    bsz, t, n = xbc.shape
    tn = 256
    return pl.pallas_call(
        _conv_kernel,
        grid=(bsz, n // tn),
        in_specs=[pl.BlockSpec((1, t, tn), lambda b, j: (b, 0, j)),
                  pl.BlockSpec((SSD_CONV, tn), lambda b, j: (0, j)),
                  pl.BlockSpec((1, tn), lambda b, j: (0, j))],
        out_specs=pl.BlockSpec((1, t, tn), lambda b, j: (b, 0, j)),
        out_shape=jax.ShapeDtypeStruct((bsz, t, n), F32),
        compiler_params=_cparams("arbitrary", "arbitrary"),
        name="ssd_conv",
    )(xbc, w, b.reshape(1, n))


def _ssd_kernel(x_f, g_f, x_b, g_b, dtb_ref, ea_ref, sel_ref, yf_ref, yb_ref, st_ref):
    @pl.when(pl.program_id(1) == 0)
    def _():
        st_ref[...] = jnp.zeros_like(st_ref)

    lane = lax.broadcasted_iota(jnp.int32, (CHUNK, SSD_W), 1)
    row = lax.broadcasted_iota(jnp.int32, (CHUNK, SSD_W), 0)
    s_idx = lane % SSD_P
    diag = s_idx == row
    ones = jnp.ones((CHUNK, CHUNK), BF16)
    half = lax.broadcasted_iota(jnp.int32, (CHUNK, 2 * SSD_P), 1) < SSD_P
    loads = []
    for d, (x_ref, g_ref, o_ref) in enumerate(((x_f, g_f, yf_ref), (x_b, g_b, yb_ref))):
        loads.append((x_ref[0, :, :SSD_W], x_ref[0, :, SSD_W:SSD_W + SSD_GN],
                      x_ref[0, :, SSD_W + SSD_GN:].astype(BF16), g_ref[0], st_ref[d], o_ref))
    outs = []
    for d, (xs, bm_f, cm, gates, st_old, o_ref) in enumerate(loads):
        bm = bm_f.astype(BF16)
        incl, _ = _tri_masks(d)
        mask_t = (s_idx <= row) if d == 0 else (s_idx >= row)
        last = CHUNK - 1 if d == 0 else 0
        dtv = _softplus(gates + dtb_ref[...])
        bcum = _dot_01(incl.astype(BF16), -dtv * ea_ref[...])
        b_col = _dot_x01(bcum, sel_ref[d])
        dt_col = _dot_x01(dtv, sel_ref[d])
        b_row = _dot_01(ones, jnp.where(diag, b_col, 0.0))
        dt_row = _dot_01(ones, jnp.where(diag, dt_col, 0.0))
        btot = b_col[last:last + 1, :]
        seg = jnp.exp(jnp.where(mask_t, b_col - b_row, -jnp.inf))
        cb = []
        for g in range(SSD_GROUPS):
            cols = slice(g * SSD_N, (g + 1) * SSD_N)
            cbg = _dot_t(cm[:, cols], bm[:, cols])
            cb.extend([jnp.concatenate([cbg, cbg], axis=1)] * (SSD_HPG // 2))
        w = (jnp.concatenate(cb, axis=1) * seg * dt_row).astype(BF16)
        xsb = xs.astype(BF16)
        st_b = st_old.astype(BF16)
        y_parts = []
        for p in range(SSD_HEADS // 2):
            cols = slice(p * 2 * SSD_P, (p + 1) * 2 * SSD_P)
            xp = xsb[:, cols]
            x_bd = jnp.concatenate([jnp.where(half, xp, 0), jnp.where(half, 0, xp)], axis=0)
            y_parts.append(_dot(w[:, cols], x_bd))
        ys_parts = [_dot(cm[:, g * SSD_N:(g + 1) * SSD_N], st_b[:, g * SSD_HPG * SSD_P:(g + 1) * SSD_HPG * SSD_P])
                    for g in range(SSD_GROUPS)]
        y = jnp.concatenate(y_parts, axis=1) + jnp.exp(b_col) * jnp.concatenate(ys_parts, axis=1)
        xe = (xs * (jnp.exp(btot - b_col) * dt_col)).astype(BF16)
        upd = [_dot(bm_f[:, g * SSD_N:(g + 1) * SSD_N].T.astype(BF16),
                    xe[:, g * SSD_HPG * SSD_P:(g + 1) * SSD_HPG * SSD_P]) for g in range(SSD_GROUPS)]
        outs.append((o_ref, y, jnp.exp(btot) * st_old + jnp.concatenate(upd, axis=1)))
    for d, (o_ref, y, st_new) in enumerate(outs):
        o_ref[0] = y
        st_ref[d] = st_new


def _ssd(xbc, gates, dtb_row, ea_row):
    bsz, t, w = xbc.shape
    nc, nctx = t // CHUNK, CTX_LEN // CHUNK
    bwd = functools.partial(_chunk_bwd, n_ctx=nctx, n_all=nc)
    return pl.pallas_call(
        _ssd_kernel,
        grid=(bsz, nc),
        in_specs=[pl.BlockSpec((1, CHUNK, w), lambda b, i: (b, i, 0)),
                  pl.BlockSpec((1, CHUNK, LANES), lambda b, i: (b, i, 0)),
                  pl.BlockSpec((1, CHUNK, w), lambda b, i: (b, bwd(i), 0)),
                  pl.BlockSpec((1, CHUNK, LANES), lambda b, i: (b, bwd(i), 0)),
                  pl.BlockSpec((1, LANES), lambda b, i: (0, 0)),
                  pl.BlockSpec((1, LANES), lambda b, i: (0, 0)),
                  pl.BlockSpec((2, LANES, SSD_W), lambda b, i: (0, 0, 0))],
        out_specs=[pl.BlockSpec((1, CHUNK, SSD_W), lambda b, i: (b, i, 0)),
                   pl.BlockSpec((1, CHUNK, SSD_W), lambda b, i: (b, bwd(i), 0))],
        out_shape=[jax.ShapeDtypeStruct((bsz, t, SSD_W), F32)] * 2,
        scratch_shapes=[pltpu.VMEM((2, SSD_N, SSD_W), F32)],
        compiler_params=_cparams("arbitrary", "arbitrary"),
        name="ssd_scan",
    )(xbc, gates, xbc, gates, dtb_row, ea_row, _ssd_head_select())


def _ssd_head_select():
    col = jnp.arange(LANES)[None, :, None]
    head = (jnp.arange(SSD_W) // SSD_P)[None, None, :]
    d = jnp.arange(2)[:, None, None]
    return (col == 4 * ML_HEADS + d * SSD_HEADS + head).astype(BF16)


def _residual_and_next(y, h_ref, g1_ref, a2_ref, b2_ref, h_out, v_out):
    h_new = h_ref[...] + g1_ref[0] * _rms(y)
    h_out[...] = h_new
    v_out[...] = _rms(h_new) * a2_ref[0] + b2_ref[0]


def _ab_out_kernel(hf_ref, hb_ref, og_ref, yf_ref, yb_ref, xs_ref, z_ref, h_ref, mln_ref, sd_ref, sn_ref,
                   w_ref, g1_ref, a2_ref, b2_ref, h_out, v_out):
    y = None
    for hd in range(ML_HEADS):
        cols = slice(hd * ML_V, (hd + 1) * ML_V)
        ml = _rms(hf_ref[:, cols] + hb_ref[:, cols]) * mln_ref[:, cols] * jax.nn.sigmoid(og_ref[:, cols])
        part = _dot(ml.astype(BF16), w_ref[cols, :])
        y = part if y is None else y + part
    ys = (yf_ref[...] + yb_ref[...] + sd_ref[...] * xs_ref[...]) * _silu(z_ref[...])
    y = y + _dot((_rms(ys) * sn_ref[...]).astype(BF16), w_ref[ML_W:, :])
    _residual_and_next(y, h_ref, g1_ref, a2_ref, b2_ref, h_out, v_out)


def _row_spec(tm, width, col=0):
    return pl.BlockSpec((tm, width), lambda i: (i, col))


def _tab_spec():
    return pl.BlockSpec((1, 1, D_MODEL), lambda i: (i, 0, 0))


def _full_spec(shape):
    return pl.BlockSpec(shape, lambda i: tuple(0 for _ in shape))


def _ab_out(hf, hb, og, yf, yb, xbc, z, h, ml_norm, ssd_d_row, ssd_norm, w_out, g1, a2, b2):
    rows = h.shape[0]
    tm = TOK_BLK
    wide = lambda: _row_spec(tm, D_MODEL)
    return pl.pallas_call(
        _ab_out_kernel,
        grid=(rows // tm,),
        in_specs=[wide(), wide(), wide(), wide(), wide(), wide(), wide(), wide(),
                  _full_spec((1, ML_W)), _full_spec((1, SSD_W)), _full_spec((1, SSD_W)),
                  _full_spec(w_out.shape), _tab_spec(), _tab_spec(), _tab_spec()],
        out_specs=[wide(), wide()],
        out_shape=[jax.ShapeDtypeStruct((rows, D_MODEL), F32)] * 2,
        compiler_params=_cparams("arbitrary"),
        name="ab_out",
    )(hf, hb, og, yf, yb, xbc, z, h, ml_norm, ssd_d_row, ssd_norm, w_out, g1, a2, b2)


def _router_gates_t(x, rw_t, rb_col):
    tm = x.shape[0]
    aff = jax.nn.sigmoid(lax.dot_general(rw_t, x, (((1,), (1,)), ((), ())), precision=HIGHEST,
                                         preferred_element_type=F32))
    sel = aff + rb_col
    row = lambda a, e: a[e:e + 1, :]
    n = EXPERTS_PER_GROUP
    best, gi = None, None
    for g in range(N_GROUPS):
        xs = [row(sel, g * n + j) for j in range(n)]
        score = None
        for a in range(n):
            for b in range(a + 1, n):
                pair = xs[a] + xs[b]
                score = pair if score is None else jnp.maximum(score, pair)
        if g == 0:
            best, gi = score, jnp.zeros((1, tm), jnp.int32)
        else:
            upd = score > best
            gi = jnp.where(upd, g, gi)
            best = jnp.where(upd, score, best)

    def pick(a, j):
        out = row(a, j)
        for g in range(1, N_GROUPS):
            out = jnp.where(gi == g, row(a, g * n + j), out)
        return out

    sel_in = [pick(sel, j) for j in range(n)]
    aff_in = [pick(aff, j) for j in range(n)]

    def argmax_first(vals):
        bv, bi = vals[0], jnp.zeros((1, tm), jnp.int32)
        for j in range(1, n):
            upd = vals[j] > bv
            bi = jnp.where(upd, j, bi)
            bv = jnp.where(upd, vals[j], bv)
        return bi

    i1 = argmax_first(sel_in)
    i2 = argmax_first([jnp.where(i1 == j, -jnp.inf, sel_in[j]) for j in range(n)])

    def take(vals, idx):
        out = vals[0]
        for j in range(1, n):
            out = jnp.where(idx == j, vals[j], out)
        return out

    w1, w2 = take(aff_in, i1), take(aff_in, i2)
    wsum = w1 + w2
    w1, w2 = w1 / wsum, w2 / wsum
    rows = [jnp.ones((1, tm), F32)]
    for e in range(N_EXPERTS):
        g, j = divmod(e, n)
        in_g = jnp.where(i1 == j, w1, 0.0) + jnp.where(i2 == j, w2, 0.0)
        rows.append(jnp.where(gi == g, in_g, 0.0))
    rows.append(jnp.zeros((LANES - 1 - N_EXPERTS, tm), F32))
    return jnp.concatenate(rows, axis=0)


def _moe_kernel(v_ref, h_ref, rw_ref, rb_ref, wg_ref, wu_ref, wd_ref, g2_ref, o_ref, acc_ref, gate_ref, xb_ref,
                *, nsub):
    e = pl.program_id(1)

    @pl.when(e == 0)
    def _():
        x = v_ref[...]
        xb_ref[...] = x.astype(BF16)
        gate_ref[...] = _router_gates_t(x, rw_ref[...], rb_ref[...]).T
        acc_ref[...] = jnp.zeros_like(acc_ref)

    lane = lax.broadcasted_iota(jnp.int32, (TOK_BLK, LANES), 1)
    for s in range(nsub):
        rows = slice(s * TOK_BLK, (s + 1) * TOK_BLK)
        xb = xb_ref[rows, :]
        gate = jnp.sum(jnp.where(lane == e, gate_ref[rows, :], 0.0), axis=1, keepdims=True)
        act = _silu(_dot(xb, wg_ref[0])) * _dot(xb, wu_ref[0]) * gate
        acc_ref[rows, :] += _dot(act.astype(BF16), wd_ref[0])

    @pl.when(e == N_EXPERTS)
    def _():
        for s in range(nsub):
            rows = slice(s * TOK_BLK, (s + 1) * TOK_BLK)
            o_ref[rows, :] = h_ref[rows, :] + g2_ref[s] * _rms(acc_ref[rows, :])


def _moe(v, h, rw_t, rb_col, wg, wu, wd, g2):
    rows = v.shape[0]
    nsub = 4
    tm = nsub * TOK_BLK
    assert rows % tm == 0
    ne = wg.shape[0]
    return pl.pallas_call(
        functools.partial(_moe_kernel, nsub=nsub),
        grid=(rows // tm, ne),
        in_specs=[pl.BlockSpec((tm, D_MODEL), lambda i, e: (i, 0)),
                  pl.BlockSpec((tm, D_MODEL), lambda i, e: (i, 0)),
                  pl.BlockSpec(rw_t.shape, lambda i, e: (0, 0)),
                  pl.BlockSpec(rb_col.shape, lambda i, e: (0, 0)),
                  pl.BlockSpec((1, D_MODEL, EXPERT_FF), lambda i, e: (e, 0, 0)),
                  pl.BlockSpec((1, D_MODEL, EXPERT_FF), lambda i, e: (e, 0, 0)),
                  pl.BlockSpec((1, EXPERT_FF, D_MODEL), lambda i, e: (e, 0, 0)),
                  pl.BlockSpec((nsub, 1, D_MODEL), lambda i, e: (i, 0, 0))],
        out_specs=pl.BlockSpec((tm, D_MODEL), lambda i, e: (i, 0)),
        out_shape=jax.ShapeDtypeStruct((rows, D_MODEL), F32),
        scratch_shapes=[pltpu.VMEM((tm, D_MODEL), F32), pltpu.VMEM((tm, LANES), F32),
                        pltpu.VMEM((tm, D_MODEL), BF16)],
        compiler_params=_cparams("arbitrary", "arbitrary"),
        name="moe",
    )(v, h, rw_t, rb_col, wg, wu, wd, g2)


def _mla_prep_kernel(m_ref, qn_ref, kvn_ref, wq_ref, wkv_ref, cq_ref, sq_ref, pq_ref, ck_ref, sk_ref, pk_ref,
                     q_ref, kn_ref, v_ref, kr_ref):
    n_nope = MLA_HEADS * MLA_NOPE
    ql = _rms(m_ref[:, :MLA_Q_RANK]) * qn_ref[...]
    q = _dot(ql.astype(BF16), wq_ref[...])
    qr = q[:, n_nope:]
    qr = qr * cq_ref[...] + jnp.dot(qr, pq_ref[...], precision=HIGHEST, preferred_element_type=F32) * sq_ref[...]
    q_ref[:, :n_nope] = (q[:, :n_nope] * MLA_SCALE).astype(q_ref.dtype)
    q_ref[:, n_nope:] = (qr * MLA_SCALE).astype(q_ref.dtype)
    kvl = _rms(m_ref[:, MLA_Q_RANK:MLA_Q_RANK + MLA_KV_RANK]) * kvn_ref[...]
    kv = _dot(kvl.astype(BF16), wkv_ref[...])
    kn_ref[...] = kv[:, :n_nope].astype(kn_ref.dtype)
    v_ref[...] = kv[:, n_nope:].astype(v_ref.dtype)
    kr = m_ref[:, MLA_Q_RANK + MLA_KV_RANK:MLA_Q_RANK + MLA_KV_RANK + MLA_ROPE]
    kr = kr * ck_ref[...] + jnp.dot(kr, pk_ref[...], precision=HIGHEST, preferred_element_type=F32) * sk_ref[...]
    kr_ref[...] = kr.astype(kr_ref.dtype)


def _rope_tables(t_len, width):
    n_lat = t_len - CTX_LEN
    pos = jnp.arange(n_lat)
    n_freq = MLA_ROPE // 4
    inv = ROPE_BASE ** (-jnp.arange(n_freq, dtype=F32) / n_freq)
    ang = jnp.concatenate([(pos // GRID_W).astype(F32)[:, None] * inv, (pos % GRID_W).astype(F32)[:, None] * inv],
                          axis=-1)
    cos = jnp.concatenate([jnp.ones((CTX_LEN, MLA_ROPE // 2), F32), jnp.cos(ang)], axis=0)
    sin = jnp.concatenate([jnp.zeros((CTX_LEN, MLA_ROPE // 2), F32), jnp.sin(ang)], axis=0)
    reps = width // MLA_ROPE
    cos_t = jnp.tile(jnp.concatenate([cos, cos], axis=1), (1, reps))
    sin_t = jnp.tile(jnp.concatenate([-sin, sin], axis=1), (1, reps))
    idx = jnp.arange(width)
    swap = (idx[:, None] == (idx[None, :] ^ (MLA_ROPE // 2))).astype(F32)
    return cos_t, sin_t, swap


def _mla_prep(m, qn, kvn, wq, wkv, bsz):
    rows, width = m.shape
    t = rows // bsz
    tm = TOK_BLK
    nb = t // tm
    n_nope, n_rope = MLA_HEADS * MLA_NOPE, MLA_HEADS * MLA_ROPE
    cq, sq, pq = _rope_tables(t, n_rope)
    ck, sk, pk = _rope_tables(t, MLA_ROPE)
    pos = lambda w: pl.BlockSpec((tm, w), lambda i: (i % nb, 0))
    return pl.pallas_call(
        _mla_prep_kernel,
        grid=(rows // tm,),
        in_specs=[_row_spec(tm, width), _full_spec(qn.shape), _full_spec(kvn.shape), _full_spec(wq.shape),
                  _full_spec(wkv.shape), pos(n_rope), pos(n_rope), _full_spec(pq.shape), pos(MLA_ROPE),
                  pos(MLA_ROPE), _full_spec(pk.shape)],
        out_specs=[_row_spec(tm, n_nope + n_rope), _row_spec(tm, n_nope), _row_spec(tm, MLA_HEADS * MLA_V),
                   _row_spec(tm, MLA_ROPE)],
        out_shape=[jax.ShapeDtypeStruct((rows, n_nope + n_rope), BF16),
                   jax.ShapeDtypeStruct((rows, n_nope), BF16),
                   jax.ShapeDtypeStruct((rows, MLA_HEADS * MLA_V), BF16),
                   jax.ShapeDtypeStruct((rows, MLA_ROPE), BF16)],
        compiler_params=_cparams("arbitrary"),
        name="mla_prep",
    )(m, qn, kvn, wq, wkv, cq, sq, pq, ck, sk, pk)


def _attn_kernel(q_ref, kn_ref, kr_ref, v_ref, o_ref):
    n_nope = MLA_HEADS * MLA_NOPE
    kr = kr_ref[0]
    for hd in range(MLA_HEADS):
        qn = q_ref[:, hd * MLA_NOPE:(hd + 1) * MLA_NOPE]
        qr = q_ref[:, n_nope + hd * MLA_ROPE:n_nope + (hd + 1) * MLA_ROPE]
        s = _dot_t(qn, kn_ref[0, :, hd * MLA_NOPE:(hd + 1) * MLA_NOPE]) + _dot_t(qr, kr)
        p = jnp.exp(s - jnp.max(s, axis=-1, keepdims=True))
        o = _dot(p.astype(BF16), v_ref[0, :, hd * MLA_V:(hd + 1) * MLA_V])
        o_ref[:, hd * MLA_V:(hd + 1) * MLA_V] = (o / jnp.sum(p, axis=-1, keepdims=True)).astype(o_ref.dtype)


def _lat_block(i, nb):
    return (i // (nb - 1)) * nb + 1 + i % (nb - 1)


def _attention(q, kn, kr, v, bsz):
    t = kn.shape[1]
    tq = TOK_BLK
    nb = t // tq
    n_lat = bsz * (nb - 1)
    kv_spec = lambda w: pl.BlockSpec((1, t, w), lambda i: (i // (nb - 1), 0, 0))
    return pl.pallas_call(
        _attn_kernel,
        grid=(n_lat,),
        in_specs=[pl.BlockSpec((tq, q.shape[1]), lambda i: (_lat_block(i, nb), 0)),
                  kv_spec(kn.shape[2]), kv_spec(kr.shape[2]), kv_spec(v.shape[2])],
        out_specs=_row_spec(tq, MLA_HEADS * MLA_V),
        out_shape=jax.ShapeDtypeStruct((n_lat * tq, MLA_HEADS * MLA_V), BF16),
        compiler_params=_cparams("arbitrary"),
        name="mla_attention",
    )(q, kn, kr, v)


def _seg_sum(x):
    i = lax.broadcasted_iota(jnp.int32, (2 * RW_N, 2 * RW_N), 0) // RW_N
    j = lax.broadcasted_iota(jnp.int32, (2 * RW_N, 2 * RW_N), 1) // RW_N
    return _dot_x01(x, (i == j).astype(BF16))


RW_GC = 6


def _rwkv_precompute(it, r_s, v_s, a_s, lw_s, kd_s, bb_s, gam_s, y0_s, phi_s, psi_s, dec_s):
    masks = [_tri_masks(0), _tri_masks(1)]
    hp = 2 * RW_N
    ch = []
    for cc in range(RW_GC):
        c = it * RW_GC + cc
        rows = pl.ds(pl.multiple_of(c * CHUNK, CHUNK), CHUNK)
        rows2 = pl.ds(pl.multiple_of(c * hp, hp), hp)
        r, v, a = r_s[rows, :], v_s[rows, :], a_s[rows, :]
        for d in range(2):
            last = CHUNK - 1 if d == 0 else 0
            lw, kd, bb = lw_s[d, rows, :], kd_s[d, rows, :], bb_s[d, rows, :]
            cl = _dot_01(masks[d][0].astype(BF16), lw)
            cl_tot = cl[last:last + 1, :]
            p_inv, p_end = jnp.exp(-cl), jnp.exp(cl_tot - cl)
            at, rt = a * jnp.exp(cl - lw), r * jnp.exp(cl)
            bt, kt, bh, kh = bb * p_inv, kd * p_inv, bb * p_end, kd * p_end
            dec_s[d, rows2, :] = jnp.broadcast_to(jnp.exp(cl_tot), (hp, hp)).T
            for hh in range(2):
                cols = slice(hh * RW_N, (hh + 1) * RW_N)
                ch.append(dict(
                    d=d, rows=rows, rows2=rows2, at=at[:, cols], rt=rt[:, cols], v=v[:, cols],
                    ar=jnp.concatenate([at[:, cols], rt[:, cols]], axis=0).astype(BF16),
                    bk=jnp.concatenate([bt[:, cols], kt[:, cols]], axis=0).astype(BF16),
                    bkt=jnp.concatenate([bh[:, cols], kh[:, cols]], axis=1).T.astype(BF16)))
    for x in ch:
        x["m4"] = _dot_t(x["ar"], x["bk"])
    for x in ch:
        incl, strict = masks[x["d"]]
        m4 = x.pop("m4")
        x["vb"] = x["v"].astype(BF16)
        x["pw"] = jnp.where(strict, m4[:CHUNK, :CHUNK], 0.0)
        x["aak"] = jnp.where(strict, m4[:CHUNK, CHUNK:], 0.0).astype(BF16)
        x["rb"] = jnp.where(incl, m4[CHUNK:, :CHUNK], 0.0).astype(BF16)
        x["rk"] = jnp.where(incl, m4[CHUNK:, CHUNK:], 0.0).astype(BF16)
    for x in ch:
        x["x"] = jnp.concatenate([_dot(x.pop("aak"), x["vb"]), x.pop("at")], axis=1)
    step = CHUNK // 2
    while True:
        for x in ch:
            x["x"] = x["x"] + _dot(x["pw"].astype(BF16), x["x"].astype(BF16))
        step //= 2
        if step == 0:
            break
        for x in ch:
            pb = x["pw"].astype(BF16)
            x["pw"] = _dot(pb, pb)
    for x in ch:
        xb = x["x"].astype(BF16)
        bkt = x.pop("bkt")
        x["rbx"] = _dot(x.pop("rb"), xb)
        x["rkv"] = _dot(x.pop("rk"), x["vb"])
        x["bx"] = _dot(bkt[:RW_N], xb)
        x["kv"] = _dot(bkt[RW_N:], x["vb"])
    zero = jnp.zeros((RW_N, RW_N), F32)
    lanes = lambda a, b: jnp.concatenate([a, b], axis=1)
    bdiag = lambda a, b: jnp.concatenate([lanes(a, zero), lanes(zero, b)], axis=0)
    for j in range(0, len(ch), 2):
        p0, p1 = ch[j], ch[j + 1]
        d, rows, rows2 = p0["d"], p0["rows"], p0["rows2"]
        gam_s[d, rows, :] = lanes(p0["rt"] + p0["rbx"][:, RW_N:], p1["rt"] + p1["rbx"][:, RW_N:]).astype(BF16)
        y0 = lanes(p0["rbx"][:, :RW_N] + p0["rkv"], p1["rbx"][:, :RW_N] + p1["rkv"])
        if d == 0:
            y0_fwd = y0
        else:
            y0_s[rows, :] = y0_fwd + y0
        phi_s[d, rows2, :] = bdiag(p0["bx"][:, RW_N:], p1["bx"][:, RW_N:]).astype(BF16)
        psi_s[d, rows2, :] = bdiag(p0["bx"][:, :RW_N] + p0["kv"], p1["bx"][:, :RW_N] + p1["kv"])


def _rwkv_step(i, states, phi_s, psi_s, dec_s, st_s, n_chunks, n_ctx):
    hp = 2 * RW_N
    loads = []
    for d in range(2):
        c = i if d == 0 else _chunk_bwd(i, n_ctx, n_chunks)
        rows2 = pl.ds(pl.multiple_of(c * hp, hp), hp)
        loads.append((rows2, phi_s[d, rows2, :], psi_s[d, rows2, :], dec_s[d, rows2, :]))
    new_states = []
    for d in range(2):
        rows2, phi, psi, dec = loads[d]
        hb = states[d].astype(BF16)
        st_s[d, rows2, :] = hb
        new_states.append(states[d] * dec + _dot(phi, hb) + psi)
    return tuple(new_states)


def _rwkv_output(it, gam_s, y0_s, st_s, bon_s, gate_s, ln_ref, o_ref):
    hp = 2 * RW_N
    items = []
    for cc in range(RW_GC):
        c = it * RW_GC + cc
        rows = pl.ds(pl.multiple_of(c * CHUNK, CHUNK), CHUNK)
        rows2 = pl.ds(pl.multiple_of(c * hp, hp), hp)
        items.append(dict(rows=rows, gam=[gam_s[d, rows, :] for d in range(2)],
                          st=[st_s[d, rows2, :] for d in range(2)], y0=y0_s[rows, :],
                          bon=bon_s[rows, :], gate=gate_s[rows, :]))
    for x in items:
        x["o"] = _dot(x["gam"][0], x["st"][0]) + _dot(x["gam"][1], x["st"][1]) + x["y0"]
    for x in items:
        x["ms"] = _seg_sum(x["o"] * x["o"]) * (1.0 / RW_N)
    for x in items:
        y = x["o"] * lax.rsqrt(x["ms"] + EPS) * ln_ref[...] + x["bon"]
        o_ref[0, x["rows"], :] = y * x["gate"]


def _rwkv_kernel(r_ref, k_ref, v_ref, low_ref, mur_ref, muk_ref, muv_ref, mul_ref, w0_ref, w2_ref, a0_ref,
                 a2_ref, g2_ref, kk_ref, ka_ref, rk_ref, ln_ref, o_ref,
                 r_s, v_s, a_s, bon_s, gate_s, lw_s, kd_s, bb_s, gam_s, y0_s, phi_s, psi_s, dec_s, st_s,
                 *, n_chunks, n_ctx):
    t_len = r_s.shape[0]
    t = lax.broadcasted_iota(jnp.int32, (t_len, 1), 0)
    lo = jnp.where(t < CTX_LEN, 0, CTX_LEN)
    hi = jnp.where(t < CTX_LEN, CTX_LEN, t_len)

    def shift_mix(x, mu):
        prev = jnp.where(t - 1 >= lo, pltpu.roll(x, shift=1, axis=0), 0.0)
        nxt = jnp.where(t + 1 < hi, pltpu.roll(x, shift=t_len - 1, axis=0), 0.0)
        return x + mu * (0.5 * (prev + nxt) - x)

    r = shift_mix(r_ref[0], mur_ref[...])
    k = shift_mix(k_ref[0], muk_ref[...])
    v = shift_mix(v_ref[0], muv_ref[...])
    low = shift_mix(low_ref[0], mul_ref[...])
    kk = k * kk_ref[...]
    kk = kk / jnp.maximum(jnp.sqrt(_seg_sum(kk * kk)), 1e-12)
    r_s[...] = r
    v_s[...] = v
    a_s[...] = -kk
    bonus = jnp.zeros_like(r)
    for d in range(2):
        wl = jnp.tanh(low[:, d * RW_DECAY_RANK:(d + 1) * RW_DECAY_RANK])
        al = low[:, 2 * RW_DECAY_RANK + d * RW_A_RANK:2 * RW_DECAY_RANK + (d + 1) * RW_A_RANK]
        wz = w0_ref[d:d + 1, :] + _dot(wl.astype(BF16), w2_ref[d])
        lw_s[d] = -jnp.exp(-_softplus(-wz) - 0.5)
        a = jax.nn.sigmoid(a0_ref[d:d + 1, :] + _dot(al.astype(BF16), a2_ref[d]))
        kd = k * (1.0 + (a - 1.0) * ka_ref[...])
        kd_s[d] = kd
        bb_s[d] = kk * a
        bonus = bonus + _seg_sum(r * kd * rk_ref[...])
    bon_s[...] = bonus * v
    gl = jax.nn.sigmoid(low[:, 2 * RW_DECAY_RANK + 2 * RW_A_RANK:])
    gate_s[...] = _dot(gl.astype(BF16), g2_ref[...])

    def precompute(it, carry):
        _rwkv_precompute(it, r_s, v_s, a_s, lw_s, kd_s, bb_s, gam_s, y0_s, phi_s, psi_s, dec_s)
        return carry

    lax.fori_loop(0, n_chunks // RW_GC, precompute, 0)
    zero = jnp.zeros((2 * RW_N, 2 * RW_N), F32)
    lax.fori_loop(0, n_chunks, lambda i, st: _rwkv_step(i, st, phi_s, psi_s, dec_s, st_s, n_chunks, n_ctx),
                  (zero, zero))

    def output(it, carry):
        _rwkv_output(it, gam_s, y0_s, st_s, bon_s, gate_s, ln_ref, o_ref)
        return carry

    lax.fori_loop(0, n_chunks // RW_GC, output, 0)


def _rwkv(rkv, low, mu, w0, w2, a0, a2, g2, kk, ka, rk, ln):
    bsz, t, _ = rkv.shape
    hp = 2 * RW_N
    n_hp = RW_W // hp
    n_low = low.shape[2]
    tok = lambda off: pl.BlockSpec((1, t, hp), lambda b, j: (b, 0, off * n_hp + j))
    vec = lambda off: pl.BlockSpec((1, hp), lambda b, j: (0, off * n_hp + j))
    par2 = pl.BlockSpec((2, hp), lambda b, j: (0, j))
    par3 = pl.BlockSpec((2, RW_DECAY_RANK, hp), lambda b, j: (0, 0, j))
    tbuf = lambda: pltpu.VMEM((t, hp), F32)
    dbuf = lambda: pltpu.VMEM((2, t, hp), F32)
    mu_rkv, mu_low = mu[None, :3 * RW_W], mu[None, 3 * RW_W:]
    row = lambda a: a.reshape(1, RW_W)
    return pl.pallas_call(
        functools.partial(_rwkv_kernel, n_chunks=t // CHUNK, n_ctx=CTX_LEN // CHUNK),
        grid=(bsz, n_hp),
        in_specs=[tok(0), tok(1), tok(2), pl.BlockSpec((1, t, n_low), lambda b, j: (b, 0, 0)),
                  vec(0), vec(1), vec(2), pl.BlockSpec((1, n_low), lambda b, j: (0, 0)),
                  par2, par3, par2, par3, pl.BlockSpec((RW_GATE_RANK, hp), lambda b, j: (0, j)),
                  vec(0), vec(0), vec(0), vec(0)],
        out_specs=pl.BlockSpec((1, t, hp), lambda b, j: (b, 0, j)),
        out_shape=jax.ShapeDtypeStruct((bsz, t, RW_W), F32),
        scratch_shapes=[tbuf(), tbuf(), tbuf(), tbuf(), tbuf(), dbuf(), dbuf(), dbuf(),
                        pltpu.VMEM((2, t, hp), BF16), tbuf(), pltpu.VMEM((2, 2 * t, hp), BF16), pltpu.VMEM((2, 2 * t, hp), F32),
                        pltpu.VMEM((2, 2 * t, hp), F32), pltpu.VMEM((2, 2 * t, hp), BF16)],
        compiler_params=_cparams("arbitrary", "arbitrary"),
        name="rwkv7",
    )(rkv, rkv, rkv, low, mu_rkv, mu_rkv, mu_rkv, mu_low, w0, w2.astype(BF16), a0, a2.astype(BF16),
      g2.astype(BF16), row(kk), row(ka), row(rk), row(ln))


def _cd_out_kernel(att_ref, rw_ref, h_ref, w_ref, g1_ref, a2_ref, b2_ref, h_out, v_out):
    n_att = MLA_HEADS * MLA_V
    y = _dot(att_ref[...], w_ref[:n_att, :]) + _dot(rw_ref[...].astype(BF16), w_ref[n_att:, :])
    _residual_and_next(y, h_ref, g1_ref, a2_ref, b2_ref, h_out, v_out)


def _cd_out(att, rw, h, w_out, g1, a2, b2, nb):
    rows = att.shape[0]
    tm = TOK_BLK
    lat = lambda: pl.BlockSpec((tm, D_MODEL), lambda i: (_lat_block(i, nb), 0))
    return pl.pallas_call(
        _cd_out_kernel,
        grid=(rows // tm,),
        in_specs=[_row_spec(tm, att.shape[1]), lat(), lat(), _full_spec(w_out.shape), _tab_spec(), _tab_spec(),
                  _tab_spec()],
        out_specs=[_row_spec(tm, D_MODEL), _row_spec(tm, D_MODEL)],
        out_shape=[jax.ShapeDtypeStruct((rows, D_MODEL), F32)] * 2,
        compiler_params=_cparams("arbitrary"),
        name="cd_out",
    )(att, rw, h, w_out, g1, a2, b2)


def _pad_cols(w, width):
    return jnp.pad(w, ((0, 0), (0, width - w.shape[1])))


def _split_cols(w, sizes):
    assert sum(sizes) == w.shape[1]
    out, start = [], 0
    for s in sizes:
        out.append(w[:, start:start + s])
        start += s
    return out


def _layer_ab(h, tabs, ab_w_in, ab_w_out, ml_i_bias, ml_f_bias, ml_norm, ssd_conv_w, ssd_conv_b, ssd_dt_bias,
              ssd_a_log, ssd_d, ssd_norm, bsz):
    a1, b1, g1, a2, b2, _ = tabs
    t = h.shape[0] // bsz
    q, k, v, og, ig, fg, z, xbc, dt = _split_cols(
        ab_w_in, (ML_QKW, ML_QKW, ML_W, ML_W, 2 * ML_HEADS, 2 * ML_HEADS, SSD_W, SSD_XBC, 2 * SSD_HEADS))
    small = _pad_cols(jnp.concatenate([ig, fg, dt], axis=1), LANES)
    w_in = jnp.concatenate([q, k, v, og, z, xbc, small], axis=1).astype(BF16)
    widths = (2 * ML_QKW + ML_W, ML_W, SSD_W, SSD_XBC, LANES)
    starts = [0]
    for wd in widths[:-1]:
        starts.append(starts[-1] + wd)
    qkv, og_p, z_p, xbc_p, small_p = _norm_proj(h, a1, b1, w_in, tuple(zip(starts, widths)),
                                               (BF16, F32, F32, F32, F32), "ab_in_proj")
    pad = LANES - 4 * ML_HEADS
    bias_row = jnp.concatenate([ml_i_bias.reshape(-1), ml_f_bias.reshape(-1), jnp.zeros((pad,), F32)])[None]
    hf, hb = _mlstm(qkv.reshape(bsz, t, -1), small_p.reshape(bsz, t, LANES), bias_row)
    xbc_c = _ssd_conv(xbc_p.reshape(bsz, t, SSD_XBC), ssd_conv_w, ssd_conv_b)
    lead = jnp.zeros((4 * ML_HEADS,), F32)
    tail = jnp.zeros((LANES - 4 * ML_HEADS - 2 * SSD_HEADS,), F32)
    dtb_row = jnp.concatenate([lead, ssd_dt_bias.reshape(-1), tail])[None]
    ea_row = jnp.concatenate([lead, jnp.exp(ssd_a_log).reshape(-1), tail])[None]
    yf, yb = _ssd(xbc_c, small_p.reshape(bsz, t, LANES), dtb_row, ea_row)
    flat = lambda a: a.reshape(bsz * t, -1)
    return _ab_out(flat(hf), flat(hb), og_p, flat(yf), flat(yb), flat(xbc_c), z_p, h, ml_norm[None],
                   jnp.repeat(ssd_d, SSD_P)[None], ssd_norm[None], ab_w_out.astype(BF16), g1, a2, b2)


def _heads_to_front(w, n_heads, first):
    k, n = w.shape
    w3 = w.reshape(k, n_heads, n // n_heads)
    return jnp.concatenate([w3[:, :, :first].reshape(k, -1), w3[:, :, first:].reshape(k, -1)], axis=1)


def _layer_cd(h, tabs_all, tabs_lat, cd_w_in, cd_w_out, mla_qn, mla_w_uq, mla_kvn, mla_w_ukv, rw_mu, rw_w0, rw_w2,
              rw_a0, rw_a2, rw_g2, rw_kk, rw_ka, rw_rk, rw_ln, bsz):
    a1, b1 = tabs_all[0], tabs_all[1]
    _, _, g1, a2, b2, _ = tabs_lat
    t = h.shape[0] // bsz
    n_mla = MLA_Q_RANK + MLA_KV_RANK + MLA_ROPE
    n_low = 2 * RW_DECAY_RANK + 2 * RW_A_RANK + RW_GATE_RANK
    mla_w = 6 * LANES
    w_in = jnp.concatenate([_pad_cols(cd_w_in[:, :n_mla], mla_w), cd_w_in[:, n_mla:]], axis=1).astype(BF16)
    segs = ((0, mla_w), (mla_w, 3 * RW_W), (mla_w + 3 * RW_W, n_low))
    m, rkv, low = _norm_proj(h, a1, b1, w_in, segs, (F32, F32, F32), "cd_in_proj")
    wq = _heads_to_front(mla_w_uq, MLA_HEADS, MLA_NOPE).astype(BF16)
    wkv = _heads_to_front(mla_w_ukv, MLA_HEADS, MLA_NOPE).astype(BF16)
    q, kn, vv, kr = _mla_prep(m, mla_qn[None], mla_kvn[None], wq, wkv, bsz)
    per_b = lambda a: a.reshape(bsz, t, -1)
    att = _attention(q, per_b(kn), per_b(kr), per_b(vv), bsz)
    rw = _rwkv(per_b(rkv), per_b(low), rw_mu, rw_w0, rw_w2, rw_a0, rw_a2, rw_g2, rw_kk, rw_ka, rw_rk, rw_ln)
    return _cd_out(att, rw.reshape(bsz * t, RW_W), h, cd_w_out.astype(BF16), g1, a2, b2, t // TOK_BLK)


def _moe_weights(router_w, router_bias, exp_w_gate, exp_w_up, exp_w_down, sh_w_gate, sh_w_up, sh_w_down):
    stack = lambda s, e: jnp.concatenate([s[None], e], axis=0).astype(BF16)
    return (router_w.T, router_bias[:, None], stack(sh_w_gate, exp_w_gate), stack(sh_w_up, exp_w_up),
            stack(sh_w_down, exp_w_down))


def kernel(x, c, ctx, c_ctx, w_mod, b_mod, norm_g, ab_w_in, ab_w_out, ml_i_bias, ml_f_bias, ml_norm, ssd_conv_w, ssd_conv_b, ssd_dt_bias, ssd_a_log, ssd_d, ssd_norm, cd_w_in, cd_w_out, mla_qn, mla_w_uq, mla_kvn, mla_w_ukv, rw_mu, rw_w0, rw_w2, rw_a0, rw_a2, rw_g2, rw_kk, rw_ka, rw_rk, rw_ln, router_w, router_bias, exp_w_gate, exp_w_up, exp_w_down, sh_w_gate, sh_w_up, sh_w_down):
    bsz, seq, _ = x.shape
    t = CTX_LEN + seq
    h = jnp.concatenate([ctx, x], axis=1).reshape(bsz * t, D_MODEL)

    mod = _modulation(c, c_ctx, w_mod[0], b_mod[0])
    tabs = _block_tables(mod, norm_g[0], bsz, t // TOK_BLK, True)
    h, v = _layer_ab(h, tabs, ab_w_in[0], ab_w_out[0], ml_i_bias[0], ml_f_bias[0], ml_norm[0], ssd_conv_w[0],
                     ssd_conv_b[0], ssd_dt_bias[0], ssd_a_log[0], ssd_d[0], ssd_norm[0], bsz)
    moe_w = _moe_weights(router_w, router_bias, exp_w_gate[0], exp_w_up[0], exp_w_down[0], sh_w_gate[0],
                         sh_w_up[0], sh_w_down[0])
    h = _moe(v, h, *moe_w, tabs[5])

    mod = _modulation(c, c_ctx, w_mod[1], b_mod[1])
    nb = t // TOK_BLK
    tabs_all = _block_tables(mod, norm_g[1], bsz, nb, True)
    tabs_lat = _block_tables(mod, norm_g[1], bsz, nb - 1, False)
    h, v = _layer_cd(h, tabs_all, tabs_lat, cd_w_in[0], cd_w_out[0], mla_qn[0], mla_w_uq[0], mla_kvn[0],
                     mla_w_ukv[0], rw_mu[0], rw_w0[0], rw_w2[0], rw_a0[0], rw_a2[0], rw_g2[0], rw_kk[0], rw_ka[0],
                     rw_rk[0], rw_ln[0], bsz)
    moe_w = _moe_weights(router_w, router_bias, exp_w_gate[1], exp_w_up[1], exp_w_down[1], sh_w_gate[1],
                         sh_w_up[1], sh_w_down[1])
    h = _moe(v, h, *moe_w, tabs_lat[5])
    return h.reshape(bsz, seq, D_MODEL)
```

```python
import functools

import jax
import jax.numpy as jnp
from jax import lax
from jax.experimental import pallas as pl
from jax.experimental.pallas import tpu as pltpu

F32 = jnp.float32
BF16 = jnp.bfloat16
HIGHEST = lax.Precision.HIGHEST

D_MODEL = 1024
CTX_LEN = 256
GRID_W = 64
EPS = 1e-6
ROPE_BASE = 10000.0
TOK_BLK = 256
CHUNK = 64
LANES = 128

ML_HEADS, ML_QK, ML_V = 4, 128, 256
ML_QKW, ML_W = ML_HEADS * ML_QK, ML_HEADS * ML_V
SSD_HEADS, SSD_P, SSD_N, SSD_GROUPS, SSD_CONV = 16, 64, 64, 2, 5
SSD_HPG = SSD_HEADS // SSD_GROUPS
SSD_W, SSD_GN = SSD_HEADS * SSD_P, SSD_GROUPS * SSD_N
SSD_XBC = SSD_W + 2 * SSD_GN
MLA_HEADS, MLA_NOPE, MLA_ROPE, MLA_V = 8, 64, 32, 64
MLA_Q_RANK, MLA_KV_RANK = 384, 256
MLA_SCALE = (MLA_NOPE + MLA_ROPE) ** -0.5
RW_HEADS, RW_N = 16, 64
RW_W = RW_HEADS * RW_N
RW_DECAY_RANK, RW_A_RANK, RW_GATE_RANK = 64, 64, 128
N_EXPERTS, N_GROUPS, EXPERT_FF = 16, 4, 512
EXPERTS_PER_GROUP = N_EXPERTS // N_GROUPS

VMEM_LIMIT = 56 * 1024 * 1024


def _cparams(*sem):
    return pltpu.CompilerParams(dimension_semantics=sem, vmem_limit_bytes=VMEM_LIMIT)


def _silu(x):
    return x * jax.nn.sigmoid(x)


def _softplus(x):
    return jnp.maximum(x, 0.0) + jnp.log1p(jnp.exp(-jnp.abs(x)))


def _log_sigmoid(x):
    return jnp.minimum(x, 0.0) - jnp.log1p(jnp.exp(-jnp.abs(x)))


def _rms(x):
    return x * lax.rsqrt(jnp.mean(x * x, axis=-1, keepdims=True) + EPS)


def _dot_t(a, b):
    return lax.dot_general(a, b, (((1,), (1,)), ((), ())), preferred_element_type=F32)


def _dot(a, b):
    return jnp.dot(a, b, preferred_element_type=F32)


def _split3(x):
    hi = x.astype(BF16)
    r1 = x - hi.astype(F32)
    mid = r1.astype(BF16)
    return hi, mid, (r1 - mid.astype(F32)).astype(BF16)


def _dot_01(m, x):
    hi, mid, lo = _split3(x)
    return _dot(m, hi) + _dot(m, mid) + _dot(m, lo)


def _dot_x01(x, m):
    hi, mid, lo = _split3(x)
    return _dot(hi, m) + _dot(mid, m) + _dot(lo, m)


def _tri_masks(d):
    row = lax.broadcasted_iota(jnp.int32, (CHUNK, CHUNK), 0)
    col = lax.broadcasted_iota(jnp.int32, (CHUNK, CHUNK), 1)
    incl = (col <= row) if d == 0 else (col >= row)
    strict = (col < row) if d == 0 else (col > row)
    return incl, strict


def _mod_kernel(s_ref, w_ref, b_ref, o_ref):
    s = _silu(s_ref[...])
    o_ref[...] = jnp.dot(s, w_ref[...], precision=HIGHEST, preferred_element_type=F32) + b_ref[...]


def _modulation(c, c_ctx, w, b):
    bsz = c.shape[0]
    rows = 8 * ((bsz + 1 + 7) // 8)
    s = jnp.zeros((rows, D_MODEL), F32).at[:bsz].set(c).at[bsz].set(c_ctx)
    n = w.shape[1]
    return pl.pallas_call(
        _mod_kernel,
        grid=(n // D_MODEL,),
        in_specs=[pl.BlockSpec((rows, D_MODEL), lambda j: (0, 0)),
                  pl.BlockSpec((D_MODEL, D_MODEL), lambda j: (0, j)),
                  pl.BlockSpec((1, D_MODEL), lambda j: (0, j))],
        out_specs=pl.BlockSpec((rows, D_MODEL), lambda j: (0, j)),
        out_shape=jax.ShapeDtypeStruct((rows, n), F32),
        compiler_params=_cparams("arbitrary"),
        name="modulation",
    )(s, w, b.reshape(1, n))


def _block_tables(mod, norm_g, bsz, blocks_per_row, with_ctx):
    j = jnp.arange(blocks_per_row)
    b = jnp.arange(bsz)
    if with_ctx:
        sel = jnp.where(j[None, :] == 0, bsz, b[:, None]).reshape(-1)
    else:
        sel = jnp.broadcast_to(b[:, None], (bsz, blocks_per_row)).reshape(-1)
    m = mod[sel]
    sh_a, sc_a, g_a, sh_f, sc_f, g_f = jnp.split(m, 6, axis=-1)
    tabs = (norm_g[0] * (1 + sc_a), sh_a, g_a * norm_g[1], norm_g[2] * (1 + sc_f), sh_f, g_f * norm_g[3])
    return tuple(t[:, None, :] for t in tabs)


def _norm_proj_kernel(x_ref, a_ref, b_ref, w_ref, *refs, segs, nsub):
    out_refs, u_ref = refs[:len(segs)], refs[len(segs)]
    for s in range(nsub):
        rows = slice(s * TOK_BLK, (s + 1) * TOK_BLK)
        u_ref[rows, :] = (_rms(x_ref[rows, :]) * a_ref[s] + b_ref[s]).astype(BF16)
    u = u_ref[...]
    for o_ref, (start, width) in zip(out_refs, segs):
        o_ref[...] = _dot(u, w_ref[:, start:start + width]).astype(o_ref.dtype)


def _norm_proj(x, a_tab, b_tab, w, segs, dtypes, name):
    rows = x.shape[0]
    nsub = 2
    tm = nsub * TOK_BLK
    assert rows % tm == 0
    n = w.shape[1]
    return pl.pallas_call(
        functools.partial(_norm_proj_kernel, segs=segs, nsub=nsub),
        grid=(rows // tm,),
        in_specs=[pl.BlockSpec((tm, D_MODEL), lambda i: (i, 0)),
                  pl.BlockSpec((nsub, 1, D_MODEL), lambda i: (i, 0, 0)),
                  pl.BlockSpec((nsub, 1, D_MODEL), lambda i: (i, 0, 0)),
                  pl.BlockSpec((D_MODEL, n), lambda i: (0, 0))],
        out_specs=[pl.BlockSpec((tm, wd), lambda i: (i, 0)) for _, wd in segs],
        out_shape=[jax.ShapeDtypeStruct((rows, wd), dt) for (_, wd), dt in zip(segs, dtypes)],
        scratch_shapes=[pltpu.VMEM((tm, D_MODEL), BF16)],
        compiler_params=_cparams("arbitrary"),
        name=name,
    )(x, a_tab, b_tab, w)


def _chunk_fwd(i):
    return i


def _chunk_bwd(i, n_ctx, n_all):
    return jnp.where(i < n_ctx, n_ctx - 1 - i, n_all + n_ctx - 1 - i)


def _mlstm_kernel(qkv_f, g_f, qkv_b, g_b, bias_ref, hf_ref, hb_ref, c_ref, n_ref, m_ref):
    @pl.when(pl.program_id(1) == 0)
    def _():
        c_ref[...] = jnp.zeros_like(c_ref)
        n_ref[...] = jnp.zeros_like(n_ref)
        m_ref[...] = jnp.zeros_like(m_ref)

    scale = ML_QK ** -0.5
    for d, (qkv_ref, g_ref, o_ref) in enumerate(((qkv_f, g_f, hf_ref), (qkv_b, g_b, hb_ref))):
        incl, _ = _tri_masks(d)
        pre = g_ref[0] + bias_ref[...]
        bcum = _dot_01(incl.astype(BF16), _log_sigmoid(pre))
        pre_t, bcum_t = pre.T, bcum.T
        last = CHUNK - 1 if d == 0 else 0
        for h in range(ML_HEADS):
            ci = d * ML_HEADS + h
            cf = 2 * ML_HEADS + ci
            b_col, b_row = bcum[:, cf:cf + 1], bcum_t[cf:cf + 1, :]
            ig_col, ig_row = pre[:, ci:ci + 1], pre_t[ci:ci + 1, :]
            btot = bcum[last:last + 1, cf:cf + 1]
            m_old = m_ref[ci:ci + 1, 0:1]
            log_d = jnp.where(incl, b_col - b_row + ig_row, -jnp.inf)
            inter = b_col + m_old
            m_j = jnp.maximum(inter, jnp.max(log_d, axis=1, keepdims=True))
            q = qkv_ref[0, :, h * ML_QK:(h + 1) * ML_QK]
            k = qkv_ref[0, :, ML_QKW + h * ML_QK:ML_QKW + (h + 1) * ML_QK]
            v = qkv_ref[0, :, 2 * ML_QKW + h * ML_V:2 * ML_QKW + (h + 1) * ML_V]
            s = _dot_t(q, k) * scale * jnp.exp(log_d - m_j)
            w_int = jnp.exp(inter - m_j)
            c_old = c_ref[ci]
            n_old = n_ref[ci]
            cq = _dot(q, c_old.astype(BF16)) * scale
            nq = jnp.sum(q.astype(F32) * n_old, axis=1, keepdims=True) * scale
            num = _dot(s.astype(BF16), v) + w_int * cq
            den = jnp.sum(s, axis=1, keepdims=True) + w_int * nq
            o_ref[0, :, h * ML_V:(h + 1) * ML_V] = num / jnp.maximum(jnp.abs(den), jnp.exp(-m_j))
            log_e_row = btot - b_row + ig_row
            log_e_col = btot - b_col + ig_col
            m_new = jnp.maximum(btot + m_old, jnp.max(log_e_row, axis=1, keepdims=True))
            e_col = jnp.exp(log_e_col - m_new)
            dec = jnp.exp(btot + m_old - m_new)
            kf = k.astype(F32)
            c_ref[ci] = dec * c_old + _dot(kf.T.astype(BF16), (v.astype(F32) * e_col).astype(BF16))
            n_ref[ci] = dec * n_old + jnp.sum(e_col * kf, axis=0, keepdims=True)
            m_ref[ci:ci + 1, :] = jnp.broadcast_to(m_new, (1, LANES))


def _mlstm(qkv, gates, bias_row):
    bsz, t, _ = qkv.shape
    nc, nctx = t // CHUNK, CTX_LEN // CHUNK
    bwd = functools.partial(_chunk_bwd, n_ctx=nctx, n_all=nc)
    w = qkv.shape[2]
    return pl.pallas_call(
        _mlstm_kernel,
        grid=(bsz, nc),
        in_specs=[pl.BlockSpec((1, CHUNK, w), lambda b, i: (b, i, 0)),
                  pl.BlockSpec((1, CHUNK, LANES), lambda b, i: (b, i, 0)),
                  pl.BlockSpec((1, CHUNK, w), lambda b, i: (b, bwd(i), 0)),
                  pl.BlockSpec((1, CHUNK, LANES), lambda b, i: (b, bwd(i), 0)),
                  pl.BlockSpec((1, LANES), lambda b, i: (0, 0))],
        out_specs=[pl.BlockSpec((1, CHUNK, ML_W), lambda b, i: (b, i, 0)),
                   pl.BlockSpec((1, CHUNK, ML_W), lambda b, i: (b, bwd(i), 0))],
        out_shape=[jax.ShapeDtypeStruct((bsz, t, ML_W), F32)] * 2,
        scratch_shapes=[pltpu.VMEM((2 * ML_HEADS, ML_QK, ML_V), F32),
                        pltpu.VMEM((2 * ML_HEADS, 1, ML_QK), F32),
                        pltpu.VMEM((2 * ML_HEADS, LANES), F32)],
        compiler_params=_cparams("arbitrary", "arbitrary"),
        name="mlstm_scan",
    )(qkv, gates, qkv, gates, bias_row)


def _conv_kernel(x_ref, w_ref, b_ref, o_ref):
    x = x_ref[0]
    t_len = x.shape[0]
    t = lax.broadcasted_iota(jnp.int32, (t_len, 1), 0)
    lo = jnp.where(t < CTX_LEN, 0, CTX_LEN)
    hi = jnp.where(t < CTX_LEN, CTX_LEN, t_len)
    half = SSD_CONV // 2
    acc = x * w_ref[half:half + 1, :]
    for tap in range(SSD_CONV):
        d = tap - half
        if d == 0:
            continue
        shifted = pltpu.roll(x, shift=(-d) % t_len, axis=0)
        ok = (t + d >= lo) & (t + d < hi)
        acc = acc + jnp.where(ok, shifted, 0.0) * w_ref[tap:tap + 1, :]
    o_ref[0] = _silu(acc + b_ref[...])


def _ssd_conv(xbc, w, b):
    bsz, t, n = xbc.shape
    tn = 256
    return pl.pallas_call(
        _conv_kernel,
        grid=(bsz, n // tn),
        in_specs=[pl.BlockSpec((1, t, tn), lambda b, j: (b, 0, j)),
                  pl.BlockSpec((SSD_CONV, tn), lambda b, j: (0, j)),
                  pl.BlockSpec((1, tn), lambda b, j: (0, j))],
        out_specs=pl.BlockSpec((1, t, tn), lambda b, j: (b, 0, j)),
        out_shape=jax.ShapeDtypeStruct((bsz, t, n), F32),
        compiler_params=_cparams("arbitrary", "arbitrary"),
        name="ssd_conv",
    )(xbc, w, b.reshape(1, n))


def _ssd_kernel(x_f, g_f, x_b, g_b, dtb_ref, ea_ref, sel_ref, yf_ref, yb_ref, st_ref):
    @pl.when(pl.program_id(1) == 0)
    def _():
        st_ref[...] = jnp.zeros_like(st_ref)

    lane = lax.broadcasted_iota(jnp.int32, (CHUNK, SSD_W), 1)
    row = lax.broadcasted_iota(jnp.int32, (CHUNK, SSD_W), 0)
    s_idx = lane % SSD_P
    diag = s_idx == row
    ones = jnp.ones((CHUNK, CHUNK), BF16)
    half = lax.broadcasted_iota(jnp.int32, (CHUNK, 2 * SSD_P), 1) < SSD_P
    loads = []
    for d, (x_ref, g_ref, o_ref) in enumerate(((x_f, g_f, yf_ref), (x_b, g_b, yb_ref))):
        loads.append((x_ref[0, :, :SSD_W], x_ref[0, :, SSD_W:SSD_W + SSD_GN],
                      x_ref[0, :, SSD_W + SSD_GN:].astype(BF16), g_ref[0], st_ref[d], o_ref))
    outs = []
    for d, (xs, bm_f, cm, gates, st_old, o_ref) in enumerate(loads):
        bm = bm_f.astype(BF16)
        incl, _ = _tri_masks(d)
        mask_t = (s_idx <= row) if d == 0 else (s_idx >= row)
        last = CHUNK - 1 if d == 0 else 0
        dtv = _softplus(gates + dtb_ref[...])
        bcum = _dot_01(incl.astype(BF16), -dtv * ea_ref[...])
        b_col = _dot_x01(bcum, sel_ref[d])
        dt_col = _dot_x01(dtv, sel_ref[d])
        b_row = _dot_01(ones, jnp.where(diag, b_col, 0.0))
        dt_row = _dot_01(ones, jnp.where(diag, dt_col, 0.0))
        btot = b_col[last:last + 1, :]
        seg = jnp.exp(jnp.where(mask_t, b_col - b_row, -jnp.inf))
        cb = []
        for g in range(SSD_GROUPS):
            cols = slice(g * SSD_N, (g + 1) * SSD_N)
            cbg = _dot_t(cm[:, cols], bm[:, cols])
            cb.extend([jnp.concatenate([cbg, cbg], axis=1)] * (SSD_HPG // 2))
        w = (jnp.concatenate(cb, axis=1) * seg * dt_row).astype(BF16)
        xsb = xs.astype(BF16)
        st_b = st_old.astype(BF16)
        y_parts = []
        for p in range(SSD_HEADS // 2):
            cols = slice(p * 2 * SSD_P, (p + 1) * 2 * SSD_P)
            xp = xsb[:, cols]
            x_bd = jnp.concatenate([jnp.where(half, xp, 0), jnp.where(half, 0, xp)], axis=0)
            y_parts.append(_dot(w[:, cols], x_bd))
        ys_parts = [_dot(cm[:, g * SSD_N:(g + 1) * SSD_N], st_b[:, g * SSD_HPG * SSD_P:(g + 1) * SSD_HPG * SSD_P])
                    for g in range(SSD_GROUPS)]
        y = jnp.concatenate(y_parts, axis=1) + jnp.exp(b_col) * jnp.concatenate(ys_parts, axis=1)
        xe = (xs * (jnp.exp(btot - b_col) * dt_col)).astype(BF16)
        upd = [_dot(bm_f[:, g * SSD_N:(g + 1) * SSD_N].T.astype(BF16),
                    xe[:, g * SSD_HPG * SSD_P:(g + 1) * SSD_HPG * SSD_P]) for g in range(SSD_GROUPS)]
        outs.append((o_ref, y, jnp.exp(btot) * st_old + jnp.concatenate(upd, axis=1)))
    for d, (o_ref, y, st_new) in enumerate(outs):
        o_ref[0] = y
        st_ref[d] = st_new


def _ssd(xbc, gates, dtb_row, ea_row):
    bsz, t, w = xbc.shape
    nc, nctx = t // CHUNK, CTX_LEN // CHUNK
    bwd = functools.partial(_chunk_bwd, n_ctx=nctx, n_all=nc)
    return pl.pallas_call(
        _ssd_kernel,
        grid=(bsz, nc),
        in_specs=[pl.BlockSpec((1, CHUNK, w), lambda b, i: (b, i, 0)),
                  pl.BlockSpec((1, CHUNK, LANES), lambda b, i: (b, i, 0)),
                  pl.BlockSpec((1, CHUNK, w), lambda b, i: (b, bwd(i), 0)),
                  pl.BlockSpec((1, CHUNK, LANES), lambda b, i: (b, bwd(i), 0)),
                  pl.BlockSpec((1, LANES), lambda b, i: (0, 0)),
                  pl.BlockSpec((1, LANES), lambda b, i: (0, 0)),
                  pl.BlockSpec((2, LANES, SSD_W), lambda b, i: (0, 0, 0))],
        out_specs=[pl.BlockSpec((1, CHUNK, SSD_W), lambda b, i: (b, i, 0)),
                   pl.BlockSpec((1, CHUNK, SSD_W), lambda b, i: (b, bwd(i), 0))],
        out_shape=[jax.ShapeDtypeStruct((bsz, t, SSD_W), F32)] * 2,
        scratch_shapes=[pltpu.VMEM((2, SSD_N, SSD_W), F32)],
        compiler_params=_cparams("arbitrary", "arbitrary"),
        name="ssd_scan",
    )(xbc, gates, xbc, gates, dtb_row, ea_row, _ssd_head_select())


def _ssd_head_select():
    col = jnp.arange(LANES)[None, :, None]
    head = (jnp.arange(SSD_W) // SSD_P)[None, None, :]
    d = jnp.arange(2)[:, None, None]
    return (col == 4 * ML_HEADS + d * SSD_HEADS + head).astype(BF16)


def _residual_and_next(y, h_ref, g1_ref, a2_ref, b2_ref, h_out, v_out):
    h_new = h_ref[...] + g1_ref[0] * _rms(y)
    h_out[...] = h_new
    v_out[...] = _rms(h_new) * a2_ref[0] + b2_ref[0]


def _ab_out_kernel(hf_ref, hb_ref, og_ref, yf_ref, yb_ref, xs_ref, z_ref, h_ref, mln_ref, sd_ref, sn_ref,
                   w_ref, g1_ref, a2_ref, b2_ref, h_out, v_out):
    y = None
    for hd in range(ML_HEADS):
        cols = slice(hd * ML_V, (hd + 1) * ML_V)
        ml = _rms(hf_ref[:, cols] + hb_ref[:, cols]) * mln_ref[:, cols] * jax.nn.sigmoid(og_ref[:, cols])
        part = _dot(ml.astype(BF16), w_ref[cols, :])
        y = part if y is None else y + part
    ys = (yf_ref[...] + yb_ref[...] + sd_ref[...] * xs_ref[...]) * _silu(z_ref[...])
    y = y + _dot((_rms(ys) * sn_ref[...]).astype(BF16), w_ref[ML_W:, :])
    _residual_and_next(y, h_ref, g1_ref, a2_ref, b2_ref, h_out, v_out)


def _row_spec(tm, width, col=0):
    return pl.BlockSpec((tm, width), lambda i: (i, col))


def _tab_spec():
    return pl.BlockSpec((1, 1, D_MODEL), lambda i: (i, 0, 0))


def _full_spec(shape):
    return pl.BlockSpec(shape, lambda i: tuple(0 for _ in shape))


def _ab_out(hf, hb, og, yf, yb, xbc, z, h, ml_norm, ssd_d_row, ssd_norm, w_out, g1, a2, b2):
    rows = h.shape[0]
    tm = TOK_BLK
    wide = lambda: _row_spec(tm, D_MODEL)
    return pl.pallas_call(
        _ab_out_kernel,
        grid=(rows // tm,),
        in_specs=[wide(), wide(), wide(), wide(), wide(), wide(), wide(), wide(),
                  _full_spec((1, ML_W)), _full_spec((1, SSD_W)), _full_spec((1, SSD_W)),
                  _full_spec(w_out.shape), _tab_spec(), _tab_spec(), _tab_spec()],
        out_specs=[wide(), wide()],
        out_shape=[jax.ShapeDtypeStruct((rows, D_MODEL), F32)] * 2,
        compiler_params=_cparams("arbitrary"),
        name="ab_out",
    )(hf, hb, og, yf, yb, xbc, z, h, ml_norm, ssd_d_row, ssd_norm, w_out, g1, a2, b2)


def _router_gates_t(x, rw_t, rb_col):
    tm = x.shape[0]
    aff = jax.nn.sigmoid(lax.dot_general(rw_t, x, (((1,), (1,)), ((), ())), precision=HIGHEST,
                                         preferred_element_type=F32))
    sel = aff + rb_col
    row = lambda a, e: a[e:e + 1, :]
    n = EXPERTS_PER_GROUP
    best, gi = None, None
    for g in range(N_GROUPS):
        xs = [row(sel, g * n + j) for j in range(n)]
        score = None
        for a in range(n):
            for b in range(a + 1, n):
                pair = xs[a] + xs[b]
                score = pair if score is None else jnp.maximum(score, pair)
        if g == 0:
            best, gi = score, jnp.zeros((1, tm), jnp.int32)
        else:
            upd = score > best
            gi = jnp.where(upd, g, gi)
            best = jnp.where(upd, score, best)

    def pick(a, j):
        out = row(a, j)
        for g in range(1, N_GROUPS):
            out = jnp.where(gi == g, row(a, g * n + j), out)
        return out

    sel_in = [pick(sel, j) for j in range(n)]
    aff_in = [pick(aff, j) for j in range(n)]

    def argmax_first(vals):
        bv, bi = vals[0], jnp.zeros((1, tm), jnp.int32)
        for j in range(1, n):
            upd = vals[j] > bv
            bi = jnp.where(upd, j, bi)
            bv = jnp.where(upd, vals[j], bv)
        return bi

    i1 = argmax_first(sel_in)
    i2 = argmax_first([jnp.where(i1 == j, -jnp.inf, sel_in[j]) for j in range(n)])

    def take(vals, idx):
        out = vals[0]
        for j in range(1, n):
            out = jnp.where(idx == j, vals[j], out)
        return out

    w1, w2 = take(aff_in, i1), take(aff_in, i2)
    wsum = w1 + w2
    w1, w2 = w1 / wsum, w2 / wsum
    rows = [jnp.ones((1, tm), F32)]
    for e in range(N_EXPERTS):
        g, j = divmod(e, n)
        in_g = jnp.where(i1 == j, w1, 0.0) + jnp.where(i2 == j, w2, 0.0)
        rows.append(jnp.where(gi == g, in_g, 0.0))
    return rows, gi


MOE_TILE = 1024
MOE_CHUNK = 288
MOE_MAX_CHUNKS = -(-MOE_TILE // MOE_CHUNK)
MOE_KEY_COL = N_EXPERTS + 1


def _moe_kernel(v_ref, h_ref, rw_ref, rb_ref, wg_ref, wu_ref, wd_ref, g2_ref, o_ref, acc_ref, gate_ref, xb_ref,
                key_ref, xg_ref, yg_ref, gg_ref, nch_ref, *, nsub):
    e = pl.program_id(1)
    tm, ck = MOE_TILE, MOE_CHUNK

    @pl.when(e == 0)
    def _():
        x = v_ref[...]
        xb = x.astype(BF16)
        xb_ref[...] = xb
        rows, gi = _router_gates_t(x, rw_ref[...], rb_ref[...])
        grp = lax.broadcasted_iota(jnp.int32, (8, tm), 0)
        member = grp == gi
        earlier = (lax.broadcasted_iota(jnp.int32, (tm, tm), 0)
                   < lax.broadcasted_iota(jnp.int32, (tm, tm), 1)).astype(BF16)
        before = _dot(member.astype(BF16), earlier)
        rank = jnp.sum(jnp.where(member, before, 0.0), axis=0, keepdims=True)
        key = gi * tm + rank.astype(jnp.int32)
        key_ref[...] = jnp.broadcast_to(key, key_ref.shape)
        rows = rows + [key.astype(F32), jnp.zeros((LANES - 2 - N_EXPERTS, tm), F32)]
        gate_ref[...] = jnp.concatenate(rows, axis=0).T
        cnt = jnp.sum(member.astype(F32), axis=1, keepdims=True)
        nch = sum(jnp.where(cnt > m * ck, 1, 0) for m in range(MOE_MAX_CHUNKS))
        for g in range(N_GROUPS):
            nch_ref[g] = nch[g, 0]
        act = _silu(_dot(xb, wg_ref[0])) * _dot(xb, wu_ref[0])
        acc_ref[...] = _dot(act.astype(BF16), wd_ref[0])

    g = jnp.maximum(e - 1, 0) // EXPERTS_PER_GROUP
    j = jnp.maximum(e - 1, 0) % EXPERTS_PER_GROUP
    n_chunks = jnp.where(e > 0, nch_ref[g], 0)
    key0 = g * tm
    chunk_rows = lambda c: pl.ds(pl.multiple_of(c * ck, 16), ck)

    def gather(c, carry):
        want = lax.broadcasted_iota(jnp.int32, (ck, tm), 0) + (key0 + c * ck)
        pick = jnp.where(key_ref[0:1, :] == want, 1.0, 0.0).astype(BF16)
        xg_ref[chunk_rows(c), :] = _dot(pick, xb_ref[...]).astype(BF16)
        gg_ref[chunk_rows(c), :] = _dot_01(pick, gate_ref[...])
        yg_ref[chunk_rows(c), :] = jnp.zeros((ck, D_MODEL), F32)
        return carry

    lax.fori_loop(0, jnp.where(j == 0, n_chunks, 0), gather, 0)

    def expert(c, carry):
        xc = xg_ref[chunk_rows(c), :]
        lane = lax.broadcasted_iota(jnp.int32, (ck, LANES), 1)
        gate = jnp.sum(jnp.where(lane == e, gg_ref[chunk_rows(c), :], 0.0), axis=1, keepdims=True)
        act = _silu(_dot(xc, wg_ref[0])) * _dot(xc, wu_ref[0]) * gate
        yg_ref[chunk_rows(c), :] += _dot(act.astype(BF16), wd_ref[0])
        return carry

    lax.fori_loop(0, n_chunks, expert, 0)

    def scatter(c, carry):
        want = lax.broadcasted_iota(jnp.int32, (tm, ck), 1) + (key0 + c * ck)
        key_col = gate_ref[:, MOE_KEY_COL:MOE_KEY_COL + 1].astype(jnp.int32)
        place = jnp.where(key_col == want, 1.0, 0.0).astype(BF16)
        acc_ref[...] += _dot(place, yg_ref[chunk_rows(c), :].astype(BF16))
        return carry

    lax.fori_loop(0, jnp.where(j == EXPERTS_PER_GROUP - 1, n_chunks, 0), scatter, 0)

    @pl.when(e == N_EXPERTS)
    def _():
        for s in range(nsub):
            rows = slice(s * TOK_BLK, (s + 1) * TOK_BLK)
            o_ref[rows, :] = h_ref[rows, :] + g2_ref[s] * _rms(acc_ref[rows, :])


def _moe(v, h, rw_t, rb_col, wg, wu, wd, g2):
    rows = v.shape[0]
    tm = MOE_TILE
    nsub = tm // TOK_BLK
    assert rows % tm == 0
    ne = wg.shape[0]
    cap = MOE_MAX_CHUNKS * MOE_CHUNK
    return pl.pallas_call(
        functools.partial(_moe_kernel, nsub=nsub),
        grid=(rows // tm, ne),
        in_specs=[pl.BlockSpec((tm, D_MODEL), lambda i, e: (i, 0)),
                  pl.BlockSpec((tm, D_MODEL), lambda i, e: (i, 0)),
                  pl.BlockSpec(rw_t.shape, lambda i, e: (0, 0)),
                  pl.BlockSpec(rb_col.shape, lambda i, e: (0, 0)),
                  pl.BlockSpec((1, D_MODEL, EXPERT_FF), lambda i, e: (e, 0, 0)),
                  pl.BlockSpec((1, D_MODEL, EXPERT_FF), lambda i, e: (e, 0, 0)),
                  pl.BlockSpec((1, EXPERT_FF, D_MODEL), lambda i, e: (e, 0, 0)),
                  pl.BlockSpec((nsub, 1, D_MODEL), lambda i, e: (i, 0, 0))],
        out_specs=pl.BlockSpec((tm, D_MODEL), lambda i, e: (i, 0)),
        out_shape=jax.ShapeDtypeStruct((rows, D_MODEL), F32),
        scratch_shapes=[pltpu.VMEM((tm, D_MODEL), F32), pltpu.VMEM((tm, LANES), F32),
                        pltpu.VMEM((tm, D_MODEL), BF16), pltpu.VMEM((8, tm), jnp.int32),
                        pltpu.VMEM((cap, D_MODEL), BF16), pltpu.VMEM((cap, D_MODEL), F32),
                        pltpu.VMEM((cap, LANES), F32), pltpu.SMEM((N_GROUPS,), jnp.int32)],
        compiler_params=_cparams("arbitrary", "arbitrary"),
        name="moe",
    )(v, h, rw_t, rb_col, wg, wu, wd, g2)


MLA_KPAD = LANES


def _rope_padded(x, cos, sin):
    n = x.shape[1] // MLA_KPAD
    lane = lax.broadcasted_iota(jnp.int32, (1, x.shape[1]), 1) % MLA_KPAD
    is_x1 = (lane >= MLA_NOPE) & (lane < MLA_NOPE + MLA_ROPE // 2)
    half = MLA_ROPE // 2
    partner = jnp.where(is_x1, pltpu.roll(x, shift=x.shape[1] - half, axis=1), pltpu.roll(x, shift=half, axis=1))
    return x * jnp.tile(cos, (1, n)) + partner * jnp.tile(sin, (1, n))


def _mla_prep_kernel(m_ref, qn_ref, kvn_ref, wq_ref, wk_ref, wv_ref, cos_ref, sin_ref, qt_ref, k_ref, vt_ref):
    cos, sin = cos_ref[...], sin_ref[...]
    ql = (_rms(m_ref[:, :MLA_Q_RANK]) * qn_ref[...]).astype(BF16)
    q = _rope_padded(_dot(ql, wq_ref[...]), cos, sin) * MLA_SCALE
    qt_ref[...] = q.T.astype(qt_ref.dtype)
    kvl = (_rms(m_ref[:, MLA_Q_RANK:MLA_Q_RANK + MLA_KV_RANK]) * kvn_ref[...]).astype(BF16)
    kr = _rope_padded(m_ref[:, MLA_Q_RANK + MLA_KV_RANK:], cos, sin)
    k_ref[...] = (_dot(kvl, wk_ref[...]) + jnp.tile(kr, (1, MLA_HEADS))).astype(k_ref.dtype)
    vt_ref[...] = _dot(kvl, wv_ref[...]).T.astype(vt_ref.dtype)


def _rope_tables(t_len):
    n_lat = t_len - CTX_LEN
    pos = jnp.arange(n_lat)
    n_freq = MLA_ROPE // 4
    inv = ROPE_BASE ** (-jnp.arange(n_freq, dtype=F32) / n_freq)
    ang = jnp.concatenate([(pos // GRID_W).astype(F32)[:, None] * inv, (pos % GRID_W).astype(F32)[:, None] * inv],
                          axis=-1)
    cos = jnp.concatenate([jnp.ones((CTX_LEN, MLA_ROPE // 2), F32), jnp.cos(ang)], axis=0)
    sin = jnp.concatenate([jnp.zeros((CTX_LEN, MLA_ROPE // 2), F32), jnp.sin(ang)], axis=0)
    tail = MLA_KPAD - MLA_NOPE - MLA_ROPE
    cos_t = jnp.concatenate([jnp.ones((t_len, MLA_NOPE), F32), cos, cos, jnp.ones((t_len, tail), F32)], axis=1)
    sin_t = jnp.concatenate([jnp.zeros((t_len, MLA_NOPE), F32), -sin, sin, jnp.zeros((t_len, tail), F32)], axis=1)
    return cos_t, sin_t


def _mla_prep(m, qn, kvn, wq, wk, wv, bsz):
    rows, width = m.shape
    t = rows // bsz
    tm = TOK_BLK
    nb = t // tm
    cos_t, sin_t = _rope_tables(t)
    pos = pl.BlockSpec((tm, MLA_KPAD), lambda i: (i % nb, 0))
    col = lambda n: pl.BlockSpec((n, tm), lambda i: (0, i))
    nk, nv = MLA_HEADS * MLA_KPAD, MLA_HEADS * MLA_V
    return pl.pallas_call(
        _mla_prep_kernel,
        grid=(rows // tm,),
        in_specs=[_row_spec(tm, width), _full_spec(qn.shape), _full_spec(kvn.shape), _full_spec(wq.shape),
                  _full_spec(wk.shape), _full_spec(wv.shape), pos, pos],
        out_specs=[col(nk), _row_spec(tm, nk), col(nv)],
        out_shape=[jax.ShapeDtypeStruct((nk, rows), BF16), jax.ShapeDtypeStruct((rows, nk), BF16),
                   jax.ShapeDtypeStruct((nv, rows), BF16)],
        compiler_params=_cparams("arbitrary"),
        name="mla_prep",
    )(m, qn, kvn, wq, wk, wv, cos_t, sin_t)


def _attn_kernel(qt_ref, k_ref, vt_ref, o_ref):
    def scores(hd):
        return _dot(k_ref[0, :, hd * MLA_KPAD:(hd + 1) * MLA_KPAD], qt_ref[hd * MLA_KPAD:(hd + 1) * MLA_KPAD, :])

    outs = []
    s_next = scores(0)
    for hd in range(MLA_HEADS):
        s = s_next
        if hd + 1 < MLA_HEADS:
            s_next = scores(hd + 1)
        p = jnp.exp(s - jnp.max(s, axis=0, keepdims=True))
        o = _dot(vt_ref[hd * MLA_V:(hd + 1) * MLA_V, :], p.astype(BF16))
        outs.append(o / jnp.sum(p, axis=0, keepdims=True))
    o_ref[...] = jnp.concatenate(outs, axis=0).T.astype(o_ref.dtype)


def _lat_block(i, nb):
    return (i // (nb - 1)) * nb + 1 + i % (nb - 1)


def _attention(qt, k, vt, bsz):
    t = k.shape[1]
    tq = TOK_BLK
    nb = t // tq
    n_lat = bsz * (nb - 1)
    return pl.pallas_call(
        _attn_kernel,
        grid=(n_lat,),
        in_specs=[pl.BlockSpec((qt.shape[0], tq), lambda i: (0, _lat_block(i, nb))),
                  pl.BlockSpec((1, t, k.shape[2]), lambda i: (i // (nb - 1), 0, 0)),
                  pl.BlockSpec((vt.shape[0], t), lambda i: (0, i // (nb - 1)))],
        out_specs=_row_spec(tq, MLA_HEADS * MLA_V),
        out_shape=jax.ShapeDtypeStruct((n_lat * tq, MLA_HEADS * MLA_V), BF16),
        compiler_params=_cparams("arbitrary"),
        name="mla_attention",
    )(qt, k, vt)


def _seg_sum(x):
    i = lax.broadcasted_iota(jnp.int32, (2 * RW_N, 2 * RW_N), 0) // RW_N
    j = lax.broadcasted_iota(jnp.int32, (2 * RW_N, 2 * RW_N), 1) // RW_N
    return _dot_x01(x, (i == j).astype(BF16))


RW_GC = 6


def _rwkv_precompute(it, r_s, v_s, a_s, lw_s, kd_s, bb_s, gam_s, y0_s, phi_s, psi_s, dec_s):
    masks = [_tri_masks(0), _tri_masks(1)]
    hp = 2 * RW_N
    ch = []
    for cc in range(RW_GC):
        c = it * RW_GC + cc
        rows = pl.ds(pl.multiple_of(c * CHUNK, CHUNK), CHUNK)
        rows2 = pl.ds(pl.multiple_of(c * hp, hp), hp)
        r, v, a = r_s[rows, :], v_s[rows, :], a_s[rows, :]
        for d in range(2):
            last = CHUNK - 1 if d == 0 else 0
            lw, kd, bb = lw_s[d, rows, :], kd_s[d, rows, :], bb_s[d, rows, :]
            cl = _dot_01(masks[d][0].astype(BF16), lw)
            cl_tot = cl[last:last + 1, :]
            p_inv, p_end = jnp.exp(-cl), jnp.exp(cl_tot - cl)
            at, rt = a * jnp.exp(cl - lw), r * jnp.exp(cl)
            bt, kt, bh, kh = bb * p_inv, kd * p_inv, bb * p_end, kd * p_end
            dec_s[d, rows2, :] = jnp.broadcast_to(jnp.exp(cl_tot), (hp, hp)).T
            for hh in range(2):
                cols = slice(hh * RW_N, (hh + 1) * RW_N)
                ch.append(dict(
                    d=d, rows=rows, rows2=rows2, at=at[:, cols], rt=rt[:, cols], v=v[:, cols],
                    ar=jnp.concatenate([at[:, cols], rt[:, cols]], axis=0).astype(BF16),
                    bk=jnp.concatenate([bt[:, cols], kt[:, cols]], axis=0).astype(BF16),
                    bkt=jnp.concatenate([bh[:, cols], kh[:, cols]], axis=1).T.astype(BF16)))
    for x in ch:
        x["m4"] = _dot_t(x["ar"], x["bk"])
    for x in ch:
        incl, strict = masks[x["d"]]
        m4 = x.pop("m4")
        x["vb"] = x["v"].astype(BF16)
        x["pw"] = jnp.where(strict, m4[:CHUNK, :CHUNK], 0.0)
        x["aak"] = jnp.where(strict, m4[:CHUNK, CHUNK:], 0.0).astype(BF16)
        x["rb"] = jnp.where(incl, m4[CHUNK:, :CHUNK], 0.0).astype(BF16)
        x["rk"] = jnp.where(incl, m4[CHUNK:, CHUNK:], 0.0).astype(BF16)
    for x in ch:
        x["x"] = jnp.concatenate([_dot(x.pop("aak"), x["vb"]), x.pop("at")], axis=1)
    step = CHUNK // 2
    while True:
        for x in ch:
            x["x"] = x["x"] + _dot(x["pw"].astype(BF16), x["x"].astype(BF16))
        step //= 2
        if step == 0:
            break
        for x in ch:
            pb = x["pw"].astype(BF16)
            x["pw"] = _dot(pb, pb)
    for x in ch:
        xb = x["x"].astype(BF16)
        bkt = x.pop("bkt")
        x["rbx"] = _dot(x.pop("rb"), xb)
        x["rkv"] = _dot(x.pop("rk"), x["vb"])
        x["bx"] = _dot(bkt[:RW_N], xb)
        x["kv"] = _dot(bkt[RW_N:], x["vb"])
    zero = jnp.zeros((RW_N, RW_N), F32)
    lanes = lambda a, b: jnp.concatenate([a, b], axis=1)
    bdiag = lambda a, b: jnp.concatenate([lanes(a, zero), lanes(zero, b)], axis=0)
    for j in range(0, len(ch), 2):
        p0, p1 = ch[j], ch[j + 1]
        d, rows, rows2 = p0["d"], p0["rows"], p0["rows2"]
        gam_s[d, rows, :] = lanes(p0["rt"] + p0["rbx"][:, RW_N:], p1["rt"] + p1["rbx"][:, RW_N:]).astype(BF16)
        y0 = lanes(p0["rbx"][:, :RW_N] + p0["rkv"], p1["rbx"][:, :RW_N] + p1["rkv"])
        if d == 0:
            y0_fwd = y0
        else:
            y0_s[rows, :] = y0_fwd + y0
        phi_s[d, rows2, :] = bdiag(p0["bx"][:, RW_N:], p1["bx"][:, RW_N:]).astype(BF16)
        psi_s[d, rows2, :] = bdiag(p0["bx"][:, :RW_N] + p0["kv"], p1["bx"][:, :RW_N] + p1["kv"])


def _rwkv_step(i, states, phi_s, psi_s, dec_s, st_s, n_chunks, n_ctx):
    hp = 2 * RW_N
    loads = []
    for d in range(2):
        c = i if d == 0 else _chunk_bwd(i, n_ctx, n_chunks)
        rows2 = pl.ds(pl.multiple_of(c * hp, hp), hp)
        loads.append((rows2, phi_s[d, rows2, :], psi_s[d, rows2, :], dec_s[d, rows2, :]))
    new_states = []
    for d in range(2):
        rows2, phi, psi, dec = loads[d]
        hb = states[d].astype(BF16)
        st_s[d, rows2, :] = hb
        new_states.append(states[d] * dec + _dot(phi, hb) + psi)
    return tuple(new_states)


def _rwkv_output(it, gam_s, y0_s, st_s, bon_s, gate_s, ln_ref, o_ref):
    hp = 2 * RW_N
    items = []
    for cc in range(RW_GC):
        c = it * RW_GC + cc
        rows = pl.ds(pl.multiple_of(c * CHUNK, CHUNK), CHUNK)
        rows2 = pl.ds(pl.multiple_of(c * hp, hp), hp)
        items.append(dict(rows=rows, gam=[gam_s[d, rows, :] for d in range(2)],
                          st=[st_s[d, rows2, :] for d in range(2)], y0=y0_s[rows, :],
                          bon=bon_s[rows, :], gate=gate_s[rows, :]))
    for x in items:
        x["o"] = _dot(x["gam"][0], x["st"][0]) + _dot(x["gam"][1], x["st"][1]) + x["y0"]
    for x in items:
        x["ms"] = _seg_sum(x["o"] * x["o"]) * (1.0 / RW_N)
    for x in items:
        y = x["o"] * lax.rsqrt(x["ms"] + EPS) * ln_ref[...] + x["bon"]
        o_ref[0, x["rows"], :] = y * x["gate"]


def _rwkv_kernel(r_ref, k_ref, v_ref, low_ref, mur_ref, muk_ref, muv_ref, mul_ref, w0_ref, w2_ref, a0_ref,
                 a2_ref, g2_ref, kk_ref, ka_ref, rk_ref, ln_ref, o_ref,
                 r_s, v_s, a_s, bon_s, gate_s, lw_s, kd_s, bb_s, gam_s, y0_s, phi_s, psi_s, dec_s, st_s,
                 *, n_chunks, n_ctx):
    t_len = r_s.shape[0]
    t = lax.broadcasted_iota(jnp.int32, (t_len, 1), 0)
    lo = jnp.where(t < CTX_LEN, 0, CTX_LEN)
    hi = jnp.where(t < CTX_LEN, CTX_LEN, t_len)

    def shift_mix(x, mu):
        prev = jnp.where(t - 1 >= lo, pltpu.roll(x, shift=1, axis=0), 0.0)
        nxt = jnp.where(t + 1 < hi, pltpu.roll(x, shift=t_len - 1, axis=0), 0.0)
        return x + mu * (0.5 * (prev + nxt) - x)

    r = shift_mix(r_ref[0], mur_ref[...])
    k = shift_mix(k_ref[0], muk_ref[...])
    v = shift_mix(v_ref[0], muv_ref[...])
    low = shift_mix(low_ref[0], mul_ref[...])
    kk = k * kk_ref[...]
    kk = kk / jnp.maximum(jnp.sqrt(_seg_sum(kk * kk)), 1e-12)
    r_s[...] = r
    v_s[...] = v
    a_s[...] = -kk
    bonus = jnp.zeros_like(r)
    for d in range(2):
        wl = jnp.tanh(low[:, d * RW_DECAY_RANK:(d + 1) * RW_DECAY_RANK])
        al = low[:, 2 * RW_DECAY_RANK + d * RW_A_RANK:2 * RW_DECAY_RANK + (d + 1) * RW_A_RANK]
        wz = w0_ref[d:d + 1, :] + _dot(wl.astype(BF16), w2_ref[d])
        lw_s[d] = -jnp.exp(-_softplus(-wz) - 0.5)
        a = jax.nn.sigmoid(a0_ref[d:d + 1, :] + _dot(al.astype(BF16), a2_ref[d]))
        kd = k * (1.0 + (a - 1.0) * ka_ref[...])
        kd_s[d] = kd
        bb_s[d] = kk * a
        bonus = bonus + _seg_sum(r * kd * rk_ref[...])
    bon_s[...] = bonus * v
    gl = jax.nn.sigmoid(low[:, 2 * RW_DECAY_RANK + 2 * RW_A_RANK:])
    gate_s[...] = _dot(gl.astype(BF16), g2_ref[...])

    def precompute(it, carry):
        _rwkv_precompute(it, r_s, v_s, a_s, lw_s, kd_s, bb_s, gam_s, y0_s, phi_s, psi_s, dec_s)
        return carry

    lax.fori_loop(0, n_chunks // RW_GC, precompute, 0)
    zero = jnp.zeros((2 * RW_N, 2 * RW_N), F32)
    lax.fori_loop(0, n_chunks, lambda i, st: _rwkv_step(i, st, phi_s, psi_s, dec_s, st_s, n_chunks, n_ctx),
                  (zero, zero))

    def output(it, carry):
        _rwkv_output(it, gam_s, y0_s, st_s, bon_s, gate_s, ln_ref, o_ref)
        return carry

    lax.fori_loop(0, n_chunks // RW_GC, output, 0)


def _rwkv(rkv, low, mu, w0, w2, a0, a2, g2, kk, ka, rk, ln):
    bsz, t, _ = rkv.shape
    hp = 2 * RW_N
    n_hp = RW_W // hp
    n_low = low.shape[2]
    tok = lambda off: pl.BlockSpec((1, t, hp), lambda b, j: (b, 0, off * n_hp + j))
    vec = lambda off: pl.BlockSpec((1, hp), lambda b, j: (0, off * n_hp + j))
    par2 = pl.BlockSpec((2, hp), lambda b, j: (0, j))
    par3 = pl.BlockSpec((2, RW_DECAY_RANK, hp), lambda b, j: (0, 0, j))
    tbuf = lambda: pltpu.VMEM((t, hp), F32)
    dbuf = lambda: pltpu.VMEM((2, t, hp), F32)
    mu_rkv, mu_low = mu[None, :3 * RW_W], mu[None, 3 * RW_W:]
    row = lambda a: a.reshape(1, RW_W)
    return pl.pallas_call(
        functools.partial(_rwkv_kernel, n_chunks=t // CHUNK, n_ctx=CTX_LEN // CHUNK),
        grid=(bsz, n_hp),
        in_specs=[tok(0), tok(1), tok(2), pl.BlockSpec((1, t, n_low), lambda b, j: (b, 0, 0)),
                  vec(0), vec(1), vec(2), pl.BlockSpec((1, n_low), lambda b, j: (0, 0)),
                  par2, par3, par2, par3, pl.BlockSpec((RW_GATE_RANK, hp), lambda b, j: (0, j)),
                  vec(0), vec(0), vec(0), vec(0)],
        out_specs=pl.BlockSpec((1, t, hp), lambda b, j: (b, 0, j)),
        out_shape=jax.ShapeDtypeStruct((bsz, t, RW_W), F32),
        scratch_shapes=[tbuf(), tbuf(), tbuf(), tbuf(), tbuf(), dbuf(), dbuf(), dbuf(),
                        pltpu.VMEM((2, t, hp), BF16), tbuf(), pltpu.VMEM((2, 2 * t, hp), BF16), pltpu.VMEM((2, 2 * t, hp), F32),
                        pltpu.VMEM((2, 2 * t, hp), F32), pltpu.VMEM((2, 2 * t, hp), BF16)],
        compiler_params=_cparams("arbitrary", "arbitrary"),
        name="rwkv7",
    )(rkv, rkv, rkv, low, mu_rkv, mu_rkv, mu_rkv, mu_low, w0, w2.astype(BF16), a0, a2.astype(BF16),
      g2.astype(BF16), row(kk), row(ka), row(rk), row(ln))


def _cd_out_kernel(att_ref, rw_ref, h_ref, w_ref, g1_ref, a2_ref, b2_ref, h_out, v_out):
    n_att = MLA_HEADS * MLA_V
    y = _dot(att_ref[...], w_ref[:n_att, :]) + _dot(rw_ref[...].astype(BF16), w_ref[n_att:, :])
    _residual_and_next(y, h_ref, g1_ref, a2_ref, b2_ref, h_out, v_out)


def _cd_out(att, rw, h, w_out, g1, a2, b2, nb):
    rows = att.shape[0]
    tm = TOK_BLK
    lat = lambda: pl.BlockSpec((tm, D_MODEL), lambda i: (_lat_block(i, nb), 0))
    return pl.pallas_call(
        _cd_out_kernel,
        grid=(rows // tm,),
        in_specs=[_row_spec(tm, att.shape[1]), lat(), lat(), _full_spec(w_out.shape), _tab_spec(), _tab_spec(),
                  _tab_spec()],
        out_specs=[_row_spec(tm, D_MODEL), _row_spec(tm, D_MODEL)],
        out_shape=[jax.ShapeDtypeStruct((rows, D_MODEL), F32)] * 2,
        compiler_params=_cparams("arbitrary"),
        name="cd_out",
    )(att, rw, h, w_out, g1, a2, b2)


def _pad_cols(w, width):
    return jnp.pad(w, ((0, 0), (0, width - w.shape[1])))


def _split_cols(w, sizes):
    assert sum(sizes) == w.shape[1]
    out, start = [], 0
    for s in sizes:
        out.append(w[:, start:start + s])
        start += s
    return out


def _layer_ab(h, tabs, ab_w_in, ab_w_out, ml_i_bias, ml_f_bias, ml_norm, ssd_conv_w, ssd_conv_b, ssd_dt_bias,
              ssd_a_log, ssd_d, ssd_norm, bsz):
    a1, b1, g1, a2, b2, _ = tabs
    t = h.shape[0] // bsz
    q, k, v, og, ig, fg, z, xbc, dt = _split_cols(
        ab_w_in, (ML_QKW, ML_QKW, ML_W, ML_W, 2 * ML_HEADS, 2 * ML_HEADS, SSD_W, SSD_XBC, 2 * SSD_HEADS))
    small = _pad_cols(jnp.concatenate([ig, fg, dt], axis=1), LANES)
    w_in = jnp.concatenate([q, k, v, og, z, xbc, small], axis=1).astype(BF16)
    widths = (2 * ML_QKW + ML_W, ML_W, SSD_W, SSD_XBC, LANES)
    starts = [0]
    for wd in widths[:-1]:
        starts.append(starts[-1] + wd)
    qkv, og_p, z_p, xbc_p, small_p = _norm_proj(h, a1, b1, w_in, tuple(zip(starts, widths)),
                                               (BF16, F32, F32, F32, F32), "ab_in_proj")
    pad = LANES - 4 * ML_HEADS
    bias_row = jnp.concatenate([ml_i_bias.reshape(-1), ml_f_bias.reshape(-1), jnp.zeros((pad,), F32)])[None]
    hf, hb = _mlstm(qkv.reshape(bsz, t, -1), small_p.reshape(bsz, t, LANES), bias_row)
    xbc_c = _ssd_conv(xbc_p.reshape(bsz, t, SSD_XBC), ssd_conv_w, ssd_conv_b)
    lead = jnp.zeros((4 * ML_HEADS,), F32)
    tail = jnp.zeros((LANES - 4 * ML_HEADS - 2 * SSD_HEADS,), F32)
    dtb_row = jnp.concatenate([lead, ssd_dt_bias.reshape(-1), tail])[None]
    ea_row = jnp.concatenate([lead, jnp.exp(ssd_a_log).reshape(-1), tail])[None]
    yf, yb = _ssd(xbc_c, small_p.reshape(bsz, t, LANES), dtb_row, ea_row)
    flat = lambda a: a.reshape(bsz * t, -1)
    return _ab_out(flat(hf), flat(hb), og_p, flat(yf), flat(yb), flat(xbc_c), z_p, h, ml_norm[None],
                   jnp.repeat(ssd_d, SSD_P)[None], ssd_norm[None], ab_w_out.astype(BF16), g1, a2, b2)


def _pad_heads(w, n_heads, take, width):
    k, n = w.shape
    w3 = w.reshape(k, n_heads, n // n_heads)[:, :, take]
    return jnp.pad(w3, ((0, 0), (0, 0), (0, width - w3.shape[2]))).reshape(k, n_heads * width)


def _layer_cd(h, tabs_all, tabs_lat, cd_w_in, cd_w_out, mla_qn, mla_w_uq, mla_kvn, mla_w_ukv, rw_mu, rw_w0, rw_w2,
              rw_a0, rw_a2, rw_g2, rw_kk, rw_ka, rw_rk, rw_ln, bsz):
    a1, b1 = tabs_all[0], tabs_all[1]
    _, _, g1, a2, b2, _ = tabs_lat
    t = h.shape[0] // bsz
    n_lat = MLA_Q_RANK + MLA_KV_RANK
    n_mla = n_lat + MLA_ROPE
    n_low = 2 * RW_DECAY_RANK + 2 * RW_A_RANK + RW_GATE_RANK
    mla_w = n_lat + MLA_KPAD
    k_rope = jnp.pad(cd_w_in[:, n_lat:n_mla], ((0, 0), (MLA_NOPE, MLA_KPAD - MLA_NOPE - MLA_ROPE)))
    w_in = jnp.concatenate([cd_w_in[:, :n_lat], k_rope, cd_w_in[:, n_mla:]], axis=1).astype(BF16)
    segs = ((0, mla_w), (mla_w, 3 * RW_W), (mla_w + 3 * RW_W, n_low))
    m, rkv, low = _norm_proj(h, a1, b1, w_in, segs, (F32, F32, F32), "cd_in_proj")
    wq = _pad_heads(mla_w_uq, MLA_HEADS, slice(None), MLA_KPAD).astype(BF16)
    wk = _pad_heads(mla_w_ukv, MLA_HEADS, slice(0, MLA_NOPE), MLA_KPAD).astype(BF16)
    wv = _pad_heads(mla_w_ukv, MLA_HEADS, slice(MLA_NOPE, None), MLA_V).astype(BF16)
    qt, k, vt = _mla_prep(m, mla_qn[None], mla_kvn[None], wq, wk, wv, bsz)
    per_b = lambda a: a.reshape(bsz, t, -1)
    att = _attention(qt, per_b(k), vt, bsz)
    rw = _rwkv(per_b(rkv), per_b(low), rw_mu, rw_w0, rw_w2, rw_a0, rw_a2, rw_g2, rw_kk, rw_ka, rw_rk, rw_ln)
    return _cd_out(att, rw.reshape(bsz * t, RW_W), h, cd_w_out.astype(BF16), g1, a2, b2, t // TOK_BLK)


def _moe_weights(router_w, router_bias, exp_w_gate, exp_w_up, exp_w_down, sh_w_gate, sh_w_up, sh_w_down):
    stack = lambda s, e: jnp.concatenate([s[None], e], axis=0).astype(BF16)
    return (router_w.T, router_bias[:, None], stack(sh_w_gate, exp_w_gate), stack(sh_w_up, exp_w_up),
            stack(sh_w_down, exp_w_down))


def kernel(x, c, ctx, c_ctx, w_mod, b_mod, norm_g, ab_w_in, ab_w_out, ml_i_bias, ml_f_bias, ml_norm, ssd_conv_w, ssd_conv_b, ssd_dt_bias, ssd_a_log, ssd_d, ssd_norm, cd_w_in, cd_w_out, mla_qn, mla_w_uq, mla_kvn, mla_w_ukv, rw_mu, rw_w0, rw_w2, rw_a0, rw_a2, rw_g2, rw_kk, rw_ka, rw_rk, rw_ln, router_w, router_bias, exp_w_gate, exp_w_up, exp_w_down, sh_w_gate, sh_w_up, sh_w_down):
    bsz, seq, _ = x.shape
    t = CTX_LEN + seq
    h = jnp.concatenate([ctx, x], axis=1).reshape(bsz * t, D_MODEL)

    mod = _modulation(c, c_ctx, w_mod[0], b_mod[0])
    tabs = _block_tables(mod, norm_g[0], bsz, t // TOK_BLK, True)
    h, v = _layer_ab(h, tabs, ab_w_in[0], ab_w_out[0], ml_i_bias[0], ml_f_bias[0], ml_norm[0], ssd_conv_w[0],
                     ssd_conv_b[0], ssd_dt_bias[0], ssd_a_log[0], ssd_d[0], ssd_norm[0], bsz)
    moe_w = _moe_weights(router_w, router_bias, exp_w_gate[0], exp_w_up[0], exp_w_down[0], sh_w_gate[0],
                         sh_w_up[0], sh_w_down[0])
    h = _moe(v, h, *moe_w, tabs[5])

    mod = _modulation(c, c_ctx, w_mod[1], b_mod[1])
    nb = t // TOK_BLK
    tabs_all = _block_tables(mod, norm_g[1], bsz, nb, True)
    tabs_lat = _block_tables(mod, norm_g[1], bsz, nb - 1, False)
    h, v = _layer_cd(h, tabs_all, tabs_lat, cd_w_in[0], cd_w_out[0], mla_qn[0], mla_w_uq[0], mla_kvn[0],
                     mla_w_ukv[0], rw_mu[0], rw_w0[0], rw_w2[0], rw_a0[0], rw_a2[0], rw_g2[0], rw_kk[0], rw_ka[0],
                     rw_rk[0], rw_ln[0], bsz)
    moe_w = _moe_weights(router_w, router_bias, exp_w_gate[1], exp_w_up[1], exp_w_down[1], sh_w_gate[1],
                         sh_w_up[1], sh_w_down[1])
    h = _moe(v, h, *moe_w, tabs_lat[5])
    return h.reshape(bsz, seq, D_MODEL)
```

```python
import functools

import jax
import jax.numpy as jnp
from jax import lax
from jax.experimental import pallas as pl
from jax.experimental.pallas import tpu as pltpu

F32 = jnp.float32
BF16 = jnp.bfloat16
HIGHEST = lax.Precision.HIGHEST

D_MODEL = 1024
CTX_LEN = 256
GRID_W = 64
EPS = 1e-6
ROPE_BASE = 10000.0
TOK_BLK = 256
CHUNK = 64
LANES = 128

ML_HEADS, ML_QK, ML_V = 4, 128, 256
ML_QKW, ML_W = ML_HEADS * ML_QK, ML_HEADS * ML_V
SSD_HEADS, SSD_P, SSD_N, SSD_GROUPS, SSD_CONV = 16, 64, 64, 2, 5
SSD_HPG = SSD_HEADS // SSD_GROUPS
SSD_W, SSD_GN = SSD_HEADS * SSD_P, SSD_GROUPS * SSD_N
SSD_XBC = SSD_W + 2 * SSD_GN
MLA_HEADS, MLA_NOPE, MLA_ROPE, MLA_V = 8, 64, 32, 64
MLA_Q_RANK, MLA_KV_RANK = 384, 256
MLA_SCALE = (MLA_NOPE + MLA_ROPE) ** -0.5
RW_HEADS, RW_N = 16, 64
RW_W = RW_HEADS * RW_N
RW_DECAY_RANK, RW_A_RANK, RW_GATE_RANK = 64, 64, 128
N_EXPERTS, N_GROUPS, EXPERT_FF = 16, 4, 512
EXPERTS_PER_GROUP = N_EXPERTS // N_GROUPS

VMEM_LIMIT = 56 * 1024 * 1024


def _cparams(*sem):
    return pltpu.CompilerParams(dimension_semantics=sem, vmem_limit_bytes=VMEM_LIMIT)


def _silu(x):
    return x * jax.nn.sigmoid(x)


def _softplus(x):
    return jnp.maximum(x, 0.0) + jnp.log1p(jnp.exp(-jnp.abs(x)))


def _log_sigmoid(x):
    return jnp.minimum(x, 0.0) - jnp.log1p(jnp.exp(-jnp.abs(x)))


def _rms(x):
    return x * lax.rsqrt(jnp.mean(x * x, axis=-1, keepdims=True) + EPS)


def _dot_t(a, b):
    return lax.dot_general(a, b, (((1,), (1,)), ((), ())), preferred_element_type=F32)


def _dot(a, b):
    return jnp.dot(a, b, preferred_element_type=F32)


def _split3(x):
    hi = x.astype(BF16)
    r1 = x - hi.astype(F32)
    mid = r1.astype(BF16)
    return hi, mid, (r1 - mid.astype(F32)).astype(BF16)


def _dot_01(m, x):
    hi, mid, lo = _split3(x)
    return _dot(m, hi) + _dot(m, mid) + _dot(m, lo)


def _dot_x01(x, m):
    hi, mid, lo = _split3(x)
    return _dot(hi, m) + _dot(mid, m) + _dot(lo, m)


def _tri_masks(d):
    row = lax.broadcasted_iota(jnp.int32, (CHUNK, CHUNK), 0)
    col = lax.broadcasted_iota(jnp.int32, (CHUNK, CHUNK), 1)
    incl = (col <= row) if d == 0 else (col >= row)
    strict = (col < row) if d == 0 else (col > row)
    return incl, strict


def _mod_kernel(s_ref, w_ref, b_ref, o_ref):
    s = _silu(s_ref[...])
    o_ref[...] = jnp.dot(s, w_ref[...], precision=HIGHEST, preferred_element_type=F32) + b_ref[...]


def _modulation(c, c_ctx, w, b):
    bsz = c.shape[0]
    rows = 8 * ((bsz + 1 + 7) // 8)
    s = jnp.zeros((rows, D_MODEL), F32).at[:bsz].set(c).at[bsz].set(c_ctx)
    n = w.shape[1]
    return pl.pallas_call(
        _mod_kernel,
        grid=(n // D_MODEL,),
        in_specs=[pl.BlockSpec((rows, D_MODEL), lambda j: (0, 0)),
                  pl.BlockSpec((D_MODEL, D_MODEL), lambda j: (0, j)),
                  pl.BlockSpec((1, D_MODEL), lambda j: (0, j))],
        out_specs=pl.BlockSpec((rows, D_MODEL), lambda j: (0, j)),
        out_shape=jax.ShapeDtypeStruct((rows, n), F32),
        compiler_params=_cparams("arbitrary"),
        name="modulation",
    )(s, w, b.reshape(1, n))


def _block_tables(mod, norm_g, bsz, blocks_per_row, with_ctx):
    j = jnp.arange(blocks_per_row)
    b = jnp.arange(bsz)
    if with_ctx:
        sel = jnp.where(j[None, :] == 0, bsz, b[:, None]).reshape(-1)
    else:
        sel = jnp.broadcast_to(b[:, None], (bsz, blocks_per_row)).reshape(-1)
    m = mod[sel]
    sh_a, sc_a, g_a, sh_f, sc_f, g_f = jnp.split(m, 6, axis=-1)
    tabs = (norm_g[0] * (1 + sc_a), sh_a, g_a * norm_g[1], norm_g[2] * (1 + sc_f), sh_f, g_f * norm_g[3])
    return tuple(t[:, None, :] for t in tabs)


def _norm_proj_kernel(x_ref, a_ref, b_ref, w_ref, *refs, segs, nsub):
    out_refs, u_ref = refs[:len(segs)], refs[len(segs)]
    for s in range(nsub):
        rows = slice(s * TOK_BLK, (s + 1) * TOK_BLK)
        u_ref[rows, :] = (_rms(x_ref[rows, :]) * a_ref[s] + b_ref[s]).astype(BF16)
    u = u_ref[...]
    for o_ref, (start, width) in zip(out_refs, segs):
        o_ref[...] = _dot(u, w_ref[:, start:start + width]).astype(o_ref.dtype)


def _norm_proj(x, a_tab, b_tab, w, segs, dtypes, name):
    rows = x.shape[0]
    nsub = 2
    tm = nsub * TOK_BLK
    assert rows % tm == 0
    n = w.shape[1]
    return pl.pallas_call(
        functools.partial(_norm_proj_kernel, segs=segs, nsub=nsub),
        grid=(rows // tm,),
        in_specs=[pl.BlockSpec((tm, D_MODEL), lambda i: (i, 0)),
                  pl.BlockSpec((nsub, 1, D_MODEL), lambda i: (i, 0, 0)),
                  pl.BlockSpec((nsub, 1, D_MODEL), lambda i: (i, 0, 0)),
                  pl.BlockSpec((D_MODEL, n), lambda i: (0, 0))],
        out_specs=[pl.BlockSpec((tm, wd), lambda i: (i, 0)) for _, wd in segs],
        out_shape=[jax.ShapeDtypeStruct((rows, wd), dt) for (_, wd), dt in zip(segs, dtypes)],
        scratch_shapes=[pltpu.VMEM((tm, D_MODEL), BF16)],
        compiler_params=_cparams("arbitrary"),
        name=name,
    )(x, a_tab, b_tab, w)


def _chunk_fwd(i):
    return i


def _chunk_bwd(i, n_ctx, n_all):
    return jnp.where(i < n_ctx, n_ctx - 1 - i, n_all + n_ctx - 1 - i)


def _mlstm_kernel(qkv_f, g_f, qkv_b, g_b, bias_ref, seli_ref, selv_ref, selq_ref, selit_ref, nmask_ref, hf_ref, hb_ref,
                  c_ref, n_ref, m_ref):
    @pl.when(pl.program_id(1) == 0)
    def _():
        c_ref[...] = jnp.zeros_like(c_ref)
        n_ref[...] = jnp.zeros_like(n_ref)
        m_ref[...] = jnp.zeros_like(m_ref)

    scale = ML_QK ** -0.5
    nch = 2 * ML_HEADS
    lane = lax.broadcasted_iota(jnp.int32, (1, LANES), 1)
    fwd_cols, valid = lane < ML_HEADS, lane < nch
    pre, bcum = [], []
    for d, g_ref in enumerate((g_f, g_b)):
        p_d = g_ref[0] + bias_ref[...]
        pre.append(p_d)
        bcum.append(_dot_01(_tri_masks(d)[0].astype(BF16), _log_sigmoid(p_d)))
    ig = jnp.where(valid, jnp.where(fwd_cols, pre[0], pre[1]), 0.0)
    b_raw = jnp.where(lane < nch + ML_HEADS, bcum[0], bcum[1])
    b_al = jnp.where(valid, pltpu.roll(b_raw, shift=LANES - nch, axis=1), 0.0)
    c_cols = ig - b_al
    row = lax.broadcasted_iota(jnp.int32, (CHUNK, LANES), 0)
    pmax, smax, sh = c_cols, c_cols, 1
    while sh < CHUNK:
        pmax = jnp.maximum(pmax, jnp.where(row >= sh, pltpu.roll(pmax, shift=sh, axis=0), -jnp.inf))
        smax = jnp.maximum(smax, jnp.where(row < CHUNK - sh, pltpu.roll(smax, shift=CHUNK - sh, axis=0), -jnp.inf))
        sh *= 2
    cmax = jnp.where(fwd_cols, pmax, smax)
    ends = lambda a: jnp.where(fwd_cols, a[CHUNK - 1:CHUNK, :], a[0:1, :])
    m_old = m_ref[0:1, :]
    m_j = b_al + jnp.maximum(m_old, cmax)
    w_int = jnp.exp(b_al + m_old - m_j)
    btot = ends(b_al)
    m_new = btot + jnp.maximum(m_old, ends(cmax))
    e_cols = jnp.exp(btot + c_cols - m_new)
    dec = jnp.exp(btot + m_old - m_new)

    seli, selv = seli_ref[...], selv_ref[...]
    lane5 = lax.broadcasted_iota(jnp.int32, (CHUNK, nch * CHUNK), 1)
    row5 = lax.broadcasted_iota(jnp.int32, (CHUNK, nch * CHUNK), 0)
    ahead = lane5 % CHUNK - row5
    causal = jnp.where(lane5 < ML_HEADS * CHUNK, ahead, -ahead) <= 0
    spread = _dot_x01(jnp.concatenate([b_al, ig, m_j], axis=0), seli)
    b_x, ig_x, mj_x = spread[:CHUNK], spread[CHUNK:2 * CHUNK], spread[2 * CHUNK:]
    c_row = _dot_01(jnp.ones((CHUNK, CHUNK), BF16), jnp.where(ahead == 0, ig_x - b_x, 0.0))
    p = jnp.exp(jnp.where(causal, b_x + c_row - mj_x, -jnp.inf))

    refs = (qkv_f, qkv_b)
    q = [refs[ci // ML_HEADS][0, :, (ci % ML_HEADS) * ML_QK:(ci % ML_HEADS + 1) * ML_QK] for ci in range(nch)]
    k = [refs[ci // ML_HEADS][0, :, ML_QKW + (ci % ML_HEADS) * ML_QK:ML_QKW + (ci % ML_HEADS + 1) * ML_QK]
         for ci in range(nch)]
    v = [refs[ci // ML_HEADS][0, :, 2 * ML_QKW + (ci % ML_HEADS) * ML_V:2 * ML_QKW + (ci % ML_HEADS + 1) * ML_V]
         for ci in range(nch)]
    c_old = [c_ref[ci] for ci in range(nch)]
    n_old = [n_ref[d] for d in range(2)]
    s_all = jnp.concatenate([_dot_t(q[ci], k[ci]) for ci in range(nch)], axis=1) * scale * p
    nq = sum(_dot(refs[d][0, :, :ML_QKW], n_old[d].astype(BF16)) for d in range(2))
    den = _dot_x01(s_all, selit_ref[...]) + w_int * nq * scale
    inv = 1.0 / jnp.maximum(jnp.abs(den), jnp.exp(-m_j))
    s_b = (s_all * _dot_x01(inv, seli)).astype(BF16)
    spread = _dot_x01(jnp.concatenate([w_int * inv * scale, e_cols], axis=0), selq_ref[...])
    dec_x = _dot_x01(jnp.broadcast_to(dec, (8, LANES)), selv)[0:1, :]
    ke_t = []
    for d in range(2):
        cols = slice(d * ML_QKW, (d + 1) * ML_QKW)
        ke_t.append((refs[d][0, :, ML_QKW:2 * ML_QKW].astype(F32) * spread[CHUNK:, cols]).T.astype(BF16))
    outs, c_new = [], []
    for ci in range(nch):
        d, h = divmod(ci, ML_HEADS)
        qw = (q[ci].astype(F32) * spread[:CHUNK, ci * ML_QK:(ci + 1) * ML_QK]).astype(BF16)
        outs.append(_dot(s_b[:, ci * CHUNK:(ci + 1) * CHUNK], v[ci]) + _dot(qw, c_old[ci].astype(BF16)))
        c_new.append(dec_x[:, ci * ML_V:(ci + 1) * ML_V] * c_old[ci]
                     + _dot(ke_t[d][h * ML_QK:(h + 1) * ML_QK, :], v[ci]))
    ones = jnp.ones((CHUNK, LANES), BF16)
    n_new = [dec * n_old[d] + jnp.where(nmask_ref[d] > 0, _dot(ke_t[d], ones), 0.0) for d in range(2)]
    hf_ref[0] = jnp.concatenate(outs[:ML_HEADS], axis=1)
    hb_ref[0] = jnp.concatenate(outs[ML_HEADS:], axis=1)
    for ci in range(nch):
        c_ref[ci] = c_new[ci]
    for d in range(2):
        n_ref[d] = n_new[d]
    m_ref[...] = jnp.broadcast_to(m_new, m_ref.shape)


def _mlstm(qkv, gates, bias_row):
    bsz, t, _ = qkv.shape
    nc, nctx = t // CHUNK, CTX_LEN // CHUNK
    bwd = functools.partial(_chunk_bwd, n_ctx=nctx, n_all=nc)
    w = qkv.shape[2]
    sel = _mlstm_selectors()
    return pl.pallas_call(
        _mlstm_kernel,
        grid=(bsz, nc),
        in_specs=[pl.BlockSpec((1, CHUNK, w), lambda b, i: (b, i, 0)),
                  pl.BlockSpec((1, CHUNK, LANES), lambda b, i: (b, i, 0)),
                  pl.BlockSpec((1, CHUNK, w), lambda b, i: (b, bwd(i), 0)),
                  pl.BlockSpec((1, CHUNK, LANES), lambda b, i: (b, bwd(i), 0)),
                  pl.BlockSpec((1, LANES), lambda b, i: (0, 0))] + [
                      pl.BlockSpec(a.shape, lambda b, i, n=a.ndim: (0,) * n) for a in sel],
        out_specs=[pl.BlockSpec((1, CHUNK, ML_W), lambda b, i: (b, i, 0)),
                   pl.BlockSpec((1, CHUNK, ML_W), lambda b, i: (b, bwd(i), 0))],
        out_shape=[jax.ShapeDtypeStruct((bsz, t, ML_W), F32)] * 2,
        scratch_shapes=[pltpu.VMEM((2 * ML_HEADS, ML_QK, ML_V), F32),
                        pltpu.VMEM((2, ML_QKW, LANES), F32),
                        pltpu.VMEM((8, LANES), F32)],
        compiler_params=_cparams("arbitrary", "arbitrary"),
        name="mlstm_scan",
    )(qkv, gates, qkv, gates, bias_row, *sel)


def _mlstm_selectors():
    nch = 2 * ML_HEADS
    col = jnp.arange(LANES)[:, None]
    seli = (col == jnp.arange(nch * CHUNK)[None, :] // CHUNK).astype(BF16)
    selv = (col == jnp.arange(nch * ML_V)[None, :] // ML_V).astype(BF16)
    head = (jnp.arange(ML_QKW) // ML_QK)[None, :, None]
    chain = jnp.arange(2)[:, None, None] * ML_HEADS + head
    nmask = (jnp.arange(LANES)[None, None, :] == chain).astype(F32)
    selq = (col == jnp.arange(nch * ML_QK)[None, :] // ML_QK).astype(BF16)
    return seli, selv, selq, seli.T, nmask


def _conv_kernel(x_ref, w_ref, b_ref, o_ref):
    x = x_ref[0]
    t_len = x.shape[0]
    t = lax.broadcasted_iota(jnp.int32, (t_len, 1), 0)
    lo = jnp.where(t < CTX_LEN, 0, CTX_LEN)
    hi = jnp.where(t < CTX_LEN, CTX_LEN, t_len)
    half = SSD_CONV // 2
    acc = x * w_ref[half:half + 1, :]
    for tap in range(SSD_CONV):
        d = tap - half
        if d == 0:
            continue
        shifted = pltpu.roll(x, shift=(-d) % t_len, axis=0)
        ok = (t + d >= lo) & (t + d < hi)
        acc = acc + jnp.where(ok, shifted, 0.0) * w_ref[tap:tap + 1, :]
    o_ref[0] = _silu(acc + b_ref[...])


def _ssd_conv(xbc, w, b):
    bsz, t, n = xbc.shape
    tn = 256
    return pl.pallas_call(
        _conv_kernel,
        grid=(bsz, n // tn),
        in_specs=[pl.BlockSpec((1, t, tn), lambda b, j: (b, 0, j)),
                  pl.BlockSpec((SSD_CONV, tn), lambda b, j: (0, j)),
                  pl.BlockSpec((1, tn), lambda b, j: (0, j))],
        out_specs=pl.BlockSpec((1, t, tn), lambda b, j: (b, 0, j)),
        out_shape=jax.ShapeDtypeStruct((bsz, t, n), F32),
        compiler_params=_cparams("arbitrary", "arbitrary"),
        name="ssd_conv",
    )(xbc, w, b.reshape(1, n))


def _ssd_kernel(x_f, g_f, x_b, g_b, dtb_ref, ea_ref, sel_ref, yf_ref, yb_ref, st_ref):
    @pl.when(pl.program_id(1) == 0)
    def _():
        st_ref[...] = jnp.zeros_like(st_ref)

    lane = lax.broadcasted_iota(jnp.int32, (CHUNK, SSD_W), 1)
    row = lax.broadcasted_iota(jnp.int32, (CHUNK, SSD_W), 0)
    s_idx = lane % SSD_P
    diag = s_idx == row
    ones = jnp.ones((CHUNK, CHUNK), BF16)
    half = lax.broadcasted_iota(jnp.int32, (CHUNK, 2 * SSD_P), 1) < SSD_P
    loads = []
    for d, (x_ref, g_ref, o_ref) in enumerate(((x_f, g_f, yf_ref), (x_b, g_b, yb_ref))):
        loads.append((x_ref[0, :, :SSD_W], x_ref[0, :, SSD_W:SSD_W + SSD_GN],
                      x_ref[0, :, SSD_W + SSD_GN:].astype(BF16), g_ref[0], st_ref[d], o_ref))
    outs = []
    for d, (xs, bm_f, cm, gates, st_old, o_ref) in enumerate(loads):
        bm = bm_f.astype(BF16)
        incl, _ = _tri_masks(d)
        mask_t = (s_idx <= row) if d == 0 else (s_idx >= row)
        last = CHUNK - 1 if d == 0 else 0
        dtv = _softplus(gates + dtb_ref[...])
        bcum = _dot_01(incl.astype(BF16), -dtv * ea_ref[...])
        b_col = _dot_x01(bcum, sel_ref[d])
        dt_col = _dot_x01(dtv, sel_ref[d])
        b_row = _dot_01(ones, jnp.where(diag, b_col, 0.0))
        dt_row = _dot_01(ones, jnp.where(diag, dt_col, 0.0))
        btot = b_col[last:last + 1, :]
        seg = jnp.exp(jnp.where(mask_t, b_col - b_row, -jnp.inf))
        cb = []
        for g in range(SSD_GROUPS):
            cols = slice(g * SSD_N, (g + 1) * SSD_N)
            cbg = _dot_t(cm[:, cols], bm[:, cols])
            cb.extend([jnp.concatenate([cbg, cbg], axis=1)] * (SSD_HPG // 2))
        w = (jnp.concatenate(cb, axis=1) * seg * dt_row).astype(BF16)
        xsb = xs.astype(BF16)
        st_b = st_old.astype(BF16)
        y_parts = []
        for p in range(SSD_HEADS // 2):
            cols = slice(p * 2 * SSD_P, (p + 1) * 2 * SSD_P)
            xp = xsb[:, cols]
            x_bd = jnp.concatenate([jnp.where(half, xp, 0), jnp.where(half, 0, xp)], axis=0)
            y_parts.append(_dot(w[:, cols], x_bd))
        ys_parts = [_dot(cm[:, g * SSD_N:(g + 1) * SSD_N], st_b[:, g * SSD_HPG * SSD_P:(g + 1) * SSD_HPG * SSD_P])
                    for g in range(SSD_GROUPS)]
        y = jnp.concatenate(y_parts, axis=1) + jnp.exp(b_col) * jnp.concatenate(ys_parts, axis=1)
        xe = (xs * (jnp.exp(btot - b_col) * dt_col)).astype(BF16)
        upd = [_dot(bm_f[:, g * SSD_N:(g + 1) * SSD_N].T.astype(BF16),
                    xe[:, g * SSD_HPG * SSD_P:(g + 1) * SSD_HPG * SSD_P]) for g in range(SSD_GROUPS)]
        outs.append((o_ref, y, jnp.exp(btot) * st_old + jnp.concatenate(upd, axis=1)))
    for d, (o_ref, y, st_new) in enumerate(outs):
        o_ref[0] = y
        st_ref[d] = st_new


def _ssd(xbc, gates, dtb_row, ea_row):
    bsz, t, w = xbc.shape
    nc, nctx = t // CHUNK, CTX_LEN // CHUNK
    bwd = functools.partial(_chunk_bwd, n_ctx=nctx, n_all=nc)
    return pl.pallas_call(
        _ssd_kernel,
        grid=(bsz, nc),
        in_specs=[pl.BlockSpec((1, CHUNK, w), lambda b, i: (b, i, 0)),
                  pl.BlockSpec((1, CHUNK, LANES), lambda b, i: (b, i, 0)),
                  pl.BlockSpec((1, CHUNK, w), lambda b, i: (b, bwd(i), 0)),
                  pl.BlockSpec((1, CHUNK, LANES), lambda b, i: (b, bwd(i), 0)),
                  pl.BlockSpec((1, LANES), lambda b, i: (0, 0)),
                  pl.BlockSpec((1, LANES), lambda b, i: (0, 0)),
                  pl.BlockSpec((2, LANES, SSD_W), lambda b, i: (0, 0, 0))],
        out_specs=[pl.BlockSpec((1, CHUNK, SSD_W), lambda b, i: (b, i, 0)),
                   pl.BlockSpec((1, CHUNK, SSD_W), lambda b, i: (b, bwd(i), 0))],
        out_shape=[jax.ShapeDtypeStruct((bsz, t, SSD_W), F32)] * 2,
        scratch_shapes=[pltpu.VMEM((2, SSD_N, SSD_W), F32)],
        compiler_params=_cparams("arbitrary", "arbitrary"),
        name="ssd_scan",
    )(xbc, gates, xbc, gates, dtb_row, ea_row, _ssd_head_select())


def _ssd_head_select():
    col = jnp.arange(LANES)[None, :, None]
    head = (jnp.arange(SSD_W) // SSD_P)[None, None, :]
    d = jnp.arange(2)[:, None, None]
    return (col == 4 * ML_HEADS + d * SSD_HEADS + head).astype(BF16)


def _residual_and_next(y, h_ref, g1_ref, a2_ref, b2_ref, h_out, v_out):
    h_new = h_ref[...] + g1_ref[0] * _rms(y)
    h_out[...] = h_new
    v_out[...] = _rms(h_new) * a2_ref[0] + b2_ref[0]


def _ab_out_kernel(hf_ref, hb_ref, og_ref, yf_ref, yb_ref, xs_ref, z_ref, h_ref, mln_ref, sd_ref, sn_ref,
                   w_ref, g1_ref, a2_ref, b2_ref, h_out, v_out):
    y = None
    for hd in range(ML_HEADS):
        cols = slice(hd * ML_V, (hd + 1) * ML_V)
        ml = _rms(hf_ref[:, cols] + hb_ref[:, cols]) * mln_ref[:, cols] * jax.nn.sigmoid(og_ref[:, cols])
        part = _dot(ml.astype(BF16), w_ref[cols, :])
        y = part if y is None else y + part
    ys = (yf_ref[...] + yb_ref[...] + sd_ref[...] * xs_ref[...]) * _silu(z_ref[...])
    y = y + _dot((_rms(ys) * sn_ref[...]).astype(BF16), w_ref[ML_W:, :])
    _residual_and_next(y, h_ref, g1_ref, a2_ref, b2_ref, h_out, v_out)


def _row_spec(tm, width, col=0):
    return pl.BlockSpec((tm, width), lambda i: (i, col))


def _tab_spec():
    return pl.BlockSpec((1, 1, D_MODEL), lambda i: (i, 0, 0))


def _full_spec(shape):
    return pl.BlockSpec(shape, lambda i: tuple(0 for _ in shape))


def _ab_out(hf, hb, og, yf, yb, xbc, z, h, ml_norm, ssd_d_row, ssd_norm, w_out, g1, a2, b2):
    rows = h.shape[0]
    tm = TOK_BLK
    wide = lambda: _row_spec(tm, D_MODEL)
    return pl.pallas_call(
        _ab_out_kernel,
        grid=(rows // tm,),
        in_specs=[wide(), wide(), wide(), wide(), wide(), wide(), wide(), wide(),
                  _full_spec((1, ML_W)), _full_spec((1, SSD_W)), _full_spec((1, SSD_W)),
                  _full_spec(w_out.shape), _tab_spec(), _tab_spec(), _tab_spec()],
        out_specs=[wide(), wide()],
        out_shape=[jax.ShapeDtypeStruct((rows, D_MODEL), F32)] * 2,
        compiler_params=_cparams("arbitrary"),
        name="ab_out",
    )(hf, hb, og, yf, yb, xbc, z, h, ml_norm, ssd_d_row, ssd_norm, w_out, g1, a2, b2)


def _router_gates_t(x, rw_t, rb_col):
    tm = x.shape[0]
    aff = jax.nn.sigmoid(lax.dot_general(rw_t, x, (((1,), (1,)), ((), ())), precision=HIGHEST,
                                         preferred_element_type=F32))
    sel = aff + rb_col
    row = lambda a, e: a[e:e + 1, :]
    n = EXPERTS_PER_GROUP
    best, gi = None, None
    for g in range(N_GROUPS):
        xs = [row(sel, g * n + j) for j in range(n)]
        score = None
        for a in range(n):
            for b in range(a + 1, n):
                pair = xs[a] + xs[b]
                score = pair if score is None else jnp.maximum(score, pair)
        if g == 0:
            best, gi = score, jnp.zeros((1, tm), jnp.int32)
        else:
            upd = score > best
            gi = jnp.where(upd, g, gi)
            best = jnp.where(upd, score, best)

    def pick(a, j):
        out = row(a, j)
        for g in range(1, N_GROUPS):
            out = jnp.where(gi == g, row(a, g * n + j), out)
        return out

    sel_in = [pick(sel, j) for j in range(n)]
    aff_in = [pick(aff, j) for j in range(n)]

    def argmax_first(vals):
        bv, bi = vals[0], jnp.zeros((1, tm), jnp.int32)
        for j in range(1, n):
            upd = vals[j] > bv
            bi = jnp.where(upd, j, bi)
            bv = jnp.where(upd, vals[j], bv)
        return bi

    i1 = argmax_first(sel_in)
    i2 = argmax_first([jnp.where(i1 == j, -jnp.inf, sel_in[j]) for j in range(n)])

    def take(vals, idx):
        out = vals[0]
        for j in range(1, n):
            out = jnp.where(idx == j, vals[j], out)
        return out

    w1, w2 = take(aff_in, i1), take(aff_in, i2)
    wsum = w1 + w2
    w1, w2 = w1 / wsum, w2 / wsum
    rows = [jnp.ones((1, tm), F32)]
    for e in range(N_EXPERTS):
        g, j = divmod(e, n)
        in_g = jnp.where(i1 == j, w1, 0.0) + jnp.where(i2 == j, w2, 0.0)
        rows.append(jnp.where(gi == g, in_g, 0.0))
    return rows, gi


MOE_TILE = 1024
MOE_CHUNK = 288
MOE_MAX_CHUNKS = -(-MOE_TILE // MOE_CHUNK)
MOE_KEY_COL = N_EXPERTS + 1


def _moe_kernel(v_ref, h_ref, rw_ref, rb_ref, wg_ref, wu_ref, wd_ref, g2_ref, o_ref, acc_ref, gate_ref, xb_ref,
                key_ref, xg_ref, yg_ref, gg_ref, nch_ref, *, nsub):
    e = pl.program_id(1)
    tm, ck = MOE_TILE, MOE_CHUNK

    @pl.when(e == 0)
    def _():
        x = v_ref[...]
        xb = x.astype(BF16)
        xb_ref[...] = xb
        rows, gi = _router_gates_t(x, rw_ref[...], rb_ref[...])
        grp = lax.broadcasted_iota(jnp.int32, (8, tm), 0)
        member = grp == gi
        earlier = (lax.broadcasted_iota(jnp.int32, (tm, tm), 0)
                   < lax.broadcasted_iota(jnp.int32, (tm, tm), 1)).astype(BF16)
        before = _dot(member.astype(BF16), earlier)
        rank = jnp.sum(jnp.where(member, before, 0.0), axis=0, keepdims=True)
        key = gi * tm + rank.astype(jnp.int32)
        key_ref[...] = jnp.broadcast_to(key, key_ref.shape)
        rows = rows + [key.astype(F32), jnp.zeros((LANES - 2 - N_EXPERTS, tm), F32)]
        gate_ref[...] = jnp.concatenate(rows, axis=0).T
        cnt = jnp.sum(member.astype(F32), axis=1, keepdims=True)
        nch = sum(jnp.where(cnt > m * ck, 1, 0) for m in range(MOE_MAX_CHUNKS))
        for g in range(N_GROUPS):
            nch_ref[g] = nch[g, 0]
        act = _silu(_dot(xb, wg_ref[0])) * _dot(xb, wu_ref[0])
        acc_ref[...] = _dot(act.astype(BF16), wd_ref[0])

    g = jnp.maximum(e - 1, 0) // EXPERTS_PER_GROUP
    j = jnp.maximum(e - 1, 0) % EXPERTS_PER_GROUP
    n_chunks = jnp.where(e > 0, nch_ref[g], 0)
    key0 = g * tm
    chunk_rows = lambda c: pl.ds(pl.multiple_of(c * ck, 16), ck)

    def gather(c, carry):
        want = lax.broadcasted_iota(jnp.int32, (ck, tm), 0) + (key0 + c * ck)
        pick = jnp.where(key_ref[0:1, :] == want, 1.0, 0.0).astype(BF16)
        xg_ref[chunk_rows(c), :] = _dot(pick, xb_ref[...]).astype(BF16)
        gg_ref[chunk_rows(c), :] = _dot_01(pick, gate_ref[...])
        yg_ref[chunk_rows(c), :] = jnp.zeros((ck, D_MODEL), F32)
        return carry

    lax.fori_loop(0, jnp.where(j == 0, n_chunks, 0), gather, 0)

    def expert(c, carry):
        xc = xg_ref[chunk_rows(c), :]
        lane = lax.broadcasted_iota(jnp.int32, (ck, LANES), 1)
        gate = jnp.sum(jnp.where(lane == e, gg_ref[chunk_rows(c), :], 0.0), axis=1, keepdims=True)
        act = _silu(_dot(xc, wg_ref[0])) * _dot(xc, wu_ref[0]) * gate
        yg_ref[chunk_rows(c), :] += _dot(act.astype(BF16), wd_ref[0])
        return carry

    lax.fori_loop(0, n_chunks, expert, 0)

    def scatter(c, carry):
        want = lax.broadcasted_iota(jnp.int32, (tm, ck), 1) + (key0 + c * ck)
        key_col = gate_ref[:, MOE_KEY_COL:MOE_KEY_COL + 1].astype(jnp.int32)
        place = jnp.where(key_col == want, 1.0, 0.0).astype(BF16)
        acc_ref[...] += _dot(place, yg_ref[chunk_rows(c), :].astype(BF16))
        return carry

    lax.fori_loop(0, jnp.where(j == EXPERTS_PER_GROUP - 1, n_chunks, 0), scatter, 0)

    @pl.when(e == N_EXPERTS)
    def _():
        for s in range(nsub):
            rows = slice(s * TOK_BLK, (s + 1) * TOK_BLK)
            o_ref[rows, :] = h_ref[rows, :] + g2_ref[s] * _rms(acc_ref[rows, :])


def _moe(v, h, rw_t, rb_col, wg, wu, wd, g2):
    rows = v.shape[0]
    tm = MOE_TILE
    nsub = tm // TOK_BLK
    assert rows % tm == 0
    ne = wg.shape[0]
    cap = MOE_MAX_CHUNKS * MOE_CHUNK
    return pl.pallas_call(
        functools.partial(_moe_kernel, nsub=nsub),
        grid=(rows // tm, ne),
        in_specs=[pl.BlockSpec((tm, D_MODEL), lambda i, e: (i, 0)),
                  pl.BlockSpec((tm, D_MODEL), lambda i, e: (i, 0)),
                  pl.BlockSpec(rw_t.shape, lambda i, e: (0, 0)),
                  pl.BlockSpec(rb_col.shape, lambda i, e: (0, 0)),
                  pl.BlockSpec((1, D_MODEL, EXPERT_FF), lambda i, e: (e, 0, 0)),
                  pl.BlockSpec((1, D_MODEL, EXPERT_FF), lambda i, e: (e, 0, 0)),
                  pl.BlockSpec((1, EXPERT_FF, D_MODEL), lambda i, e: (e, 0, 0)),
                  pl.BlockSpec((nsub, 1, D_MODEL), lambda i, e: (i, 0, 0))],
        out_specs=pl.BlockSpec((tm, D_MODEL), lambda i, e: (i, 0)),
        out_shape=jax.ShapeDtypeStruct((rows, D_MODEL), F32),
        scratch_shapes=[pltpu.VMEM((tm, D_MODEL), F32), pltpu.VMEM((tm, LANES), F32),
                        pltpu.VMEM((tm, D_MODEL), BF16), pltpu.VMEM((8, tm), jnp.int32),
                        pltpu.VMEM((cap, D_MODEL), BF16), pltpu.VMEM((cap, D_MODEL), F32),
                        pltpu.VMEM((cap, LANES), F32), pltpu.SMEM((N_GROUPS,), jnp.int32)],
        compiler_params=_cparams("arbitrary", "arbitrary"),
        name="moe",
    )(v, h, rw_t, rb_col, wg, wu, wd, g2)


MLA_KPAD = LANES


def _rope_padded(x, cos, sin):
    n = x.shape[1] // MLA_KPAD
    lane = lax.broadcasted_iota(jnp.int32, (1, x.shape[1]), 1) % MLA_KPAD
    is_x1 = (lane >= MLA_NOPE) & (lane < MLA_NOPE + MLA_ROPE // 2)
    half = MLA_ROPE // 2
    partner = jnp.where(is_x1, pltpu.roll(x, shift=x.shape[1] - half, axis=1), pltpu.roll(x, shift=half, axis=1))
    return x * jnp.tile(cos, (1, n)) + partner * jnp.tile(sin, (1, n))


def _mla_prep_kernel(m_ref, qn_ref, kvn_ref, wq_ref, wk_ref, wv_ref, cos_ref, sin_ref, qt_ref, k_ref, vt_ref):
    cos, sin = cos_ref[...], sin_ref[...]
    ql = (_rms(m_ref[:, :MLA_Q_RANK]) * qn_ref[...]).astype(BF16)
    q = _rope_padded(_dot(ql, wq_ref[...]), cos, sin) * MLA_SCALE
    qt_ref[...] = q.T.astype(qt_ref.dtype)
    kvl = (_rms(m_ref[:, MLA_Q_RANK:MLA_Q_RANK + MLA_KV_RANK]) * kvn_ref[...]).astype(BF16)
    kr = _rope_padded(m_ref[:, MLA_Q_RANK + MLA_KV_RANK:], cos, sin)
    k_ref[...] = (_dot(kvl, wk_ref[...]) + jnp.tile(kr, (1, MLA_HEADS))).astype(k_ref.dtype)
    vt_ref[...] = _dot(kvl, wv_ref[...]).T.astype(vt_ref.dtype)


def _rope_tables(t_len):
    n_lat = t_len - CTX_LEN
    pos = jnp.arange(n_lat)
    n_freq = MLA_ROPE // 4
    inv = ROPE_BASE ** (-jnp.arange(n_freq, dtype=F32) / n_freq)
    ang = jnp.concatenate([(pos // GRID_W).astype(F32)[:, None] * inv, (pos % GRID_W).astype(F32)[:, None] * inv],
                          axis=-1)
    cos = jnp.concatenate([jnp.ones((CTX_LEN, MLA_ROPE // 2), F32), jnp.cos(ang)], axis=0)
    sin = jnp.concatenate([jnp.zeros((CTX_LEN, MLA_ROPE // 2), F32), jnp.sin(ang)], axis=0)
    tail = MLA_KPAD - MLA_NOPE - MLA_ROPE
    cos_t = jnp.concatenate([jnp.ones((t_len, MLA_NOPE), F32), cos, cos, jnp.ones((t_len, tail), F32)], axis=1)
    sin_t = jnp.concatenate([jnp.zeros((t_len, MLA_NOPE), F32), -sin, sin, jnp.zeros((t_len, tail), F32)], axis=1)
    return cos_t, sin_t


def _mla_prep(m, qn, kvn, wq, wk, wv, bsz):
    rows, width = m.shape
    t = rows // bsz
    tm = TOK_BLK
    nb = t // tm
    cos_t, sin_t = _rope_tables(t)
    pos = pl.BlockSpec((tm, MLA_KPAD), lambda i: (i % nb, 0))
    col = lambda n: pl.BlockSpec((n, tm), lambda i: (0, i))
    nk, nv = MLA_HEADS * MLA_KPAD, MLA_HEADS * MLA_V
    return pl.pallas_call(
        _mla_prep_kernel,
        grid=(rows // tm,),
        in_specs=[_row_spec(tm, width), _full_spec(qn.shape), _full_spec(kvn.shape), _full_spec(wq.shape),
                  _full_spec(wk.shape), _full_spec(wv.shape), pos, pos],
        out_specs=[col(nk), _row_spec(tm, nk), col(nv)],
        out_shape=[jax.ShapeDtypeStruct((nk, rows), BF16), jax.ShapeDtypeStruct((rows, nk), BF16),
                   jax.ShapeDtypeStruct((nv, rows), BF16)],
        compiler_params=_cparams("arbitrary"),
        name="mla_prep",
    )(m, qn, kvn, wq, wk, wv, cos_t, sin_t)


def _attn_kernel(qt_ref, k_ref, vt_ref, o_ref):
    def scores(hd):
        return _dot(k_ref[0, :, hd * MLA_KPAD:(hd + 1) * MLA_KPAD], qt_ref[hd * MLA_KPAD:(hd + 1) * MLA_KPAD, :])

    outs = []
    s_next = scores(0)
    for hd in range(MLA_HEADS):
        s = s_next
        if hd + 1 < MLA_HEADS:
            s_next = scores(hd + 1)
        p = jnp.exp(s - jnp.max(s, axis=0, keepdims=True))
        o = _dot(vt_ref[hd * MLA_V:(hd + 1) * MLA_V, :], p.astype(BF16))
        outs.append(o / jnp.sum(p, axis=0, keepdims=True))
    o_ref[...] = jnp.concatenate(outs, axis=0).T.astype(o_ref.dtype)


def _lat_block(i, nb):
    return (i // (nb - 1)) * nb + 1 + i % (nb - 1)


def _attention(qt, k, vt, bsz):
    t = k.shape[1]
    tq = TOK_BLK
    nb = t // tq
    n_lat = bsz * (nb - 1)
    return pl.pallas_call(
        _attn_kernel,
        grid=(n_lat,),
        in_specs=[pl.BlockSpec((qt.shape[0], tq), lambda i: (0, _lat_block(i, nb))),
                  pl.BlockSpec((1, t, k.shape[2]), lambda i: (i // (nb - 1), 0, 0)),
                  pl.BlockSpec((vt.shape[0], t), lambda i: (0, i // (nb - 1)))],
        out_specs=_row_spec(tq, MLA_HEADS * MLA_V),
        out_shape=jax.ShapeDtypeStruct((n_lat * tq, MLA_HEADS * MLA_V), BF16),
        compiler_params=_cparams("arbitrary"),
        name="mla_attention",
    )(qt, k, vt)


def _seg_sum(x):
    i = lax.broadcasted_iota(jnp.int32, (2 * RW_N, 2 * RW_N), 0) // RW_N
    j = lax.broadcasted_iota(jnp.int32, (2 * RW_N, 2 * RW_N), 1) // RW_N
    return _dot_x01(x, (i == j).astype(BF16))


RW_GC = 6


def _rwkv_precompute(it, r_s, v_s, a_s, lw_s, kd_s, bb_s, gam_s, y0_s, phi_s, psi_s, dec_s):
    masks = [_tri_masks(0), _tri_masks(1)]
    hp = 2 * RW_N
    ch = []
    for cc in range(RW_GC):
        c = it * RW_GC + cc
        rows = pl.ds(pl.multiple_of(c * CHUNK, CHUNK), CHUNK)
        rows2 = pl.ds(pl.multiple_of(c * hp, hp), hp)
        r, v, a = r_s[rows, :], v_s[rows, :], a_s[rows, :]
        for d in range(2):
            last = CHUNK - 1 if d == 0 else 0
            lw, kd, bb = lw_s[d, rows, :], kd_s[d, rows, :], bb_s[d, rows, :]
            cl = _dot_01(masks[d][0].astype(BF16), lw)
            cl_tot = cl[last:last + 1, :]
            p_inv, p_end = jnp.exp(-cl), jnp.exp(cl_tot - cl)
            at, rt = a * jnp.exp(cl - lw), r * jnp.exp(cl)
            bt, kt, bh, kh = bb * p_inv, kd * p_inv, bb * p_end, kd * p_end
            dec_s[d, rows2, :] = jnp.broadcast_to(jnp.exp(cl_tot), (hp, hp)).T
            for hh in range(2):
                cols = slice(hh * RW_N, (hh + 1) * RW_N)
                ch.append(dict(
                    d=d, rows=rows, rows2=rows2, at=at[:, cols], rt=rt[:, cols], v=v[:, cols],
                    ar=jnp.concatenate([at[:, cols], rt[:, cols]], axis=0).astype(BF16),
                    bk=jnp.concatenate([bt[:, cols], kt[:, cols]], axis=0).astype(BF16),
                    bkt=jnp.concatenate([bh[:, cols], kh[:, cols]], axis=1).T.astype(BF16)))
    for x in ch:
        x["m4"] = _dot_t(x["ar"], x["bk"])
    for x in ch:
        incl, strict = masks[x["d"]]
        m4 = x.pop("m4")
        x["vb"] = x["v"].astype(BF16)
        x["pw"] = jnp.where(strict, m4[:CHUNK, :CHUNK], 0.0)
        x["aak"] = jnp.where(strict, m4[:CHUNK, CHUNK:], 0.0).astype(BF16)
        x["rb"] = jnp.where(incl, m4[CHUNK:, :CHUNK], 0.0).astype(BF16)
        x["rk"] = jnp.where(incl, m4[CHUNK:, CHUNK:], 0.0).astype(BF16)
    for x in ch:
        x["x"] = jnp.concatenate([_dot(x.pop("aak"), x["vb"]), x.pop("at")], axis=1)
    step = CHUNK // 2
    while True:
        for x in ch:
            x["x"] = x["x"] + _dot(x["pw"].astype(BF16), x["x"].astype(BF16))
        step //= 2
        if step == 0:
            break
        for x in ch:
            pb = x["pw"].astype(BF16)
            x["pw"] = _dot(pb, pb)
    for x in ch:
        xb = x["x"].astype(BF16)
        bkt = x.pop("bkt")
        x["rbx"] = _dot(x.pop("rb"), xb)
        x["rkv"] = _dot(x.pop("rk"), x["vb"])
        x["bx"] = _dot(bkt[:RW_N], xb)
        x["kv"] = _dot(bkt[RW_N:], x["vb"])
    zero = jnp.zeros((RW_N, RW_N), F32)
    lanes = lambda a, b: jnp.concatenate([a, b], axis=1)
    bdiag = lambda a, b: jnp.concatenate([lanes(a, zero), lanes(zero, b)], axis=0)
    for j in range(0, len(ch), 2):
        p0, p1 = ch[j], ch[j + 1]
        d, rows, rows2 = p0["d"], p0["rows"], p0["rows2"]
        gam_s[d, rows, :] = lanes(p0["rt"] + p0["rbx"][:, RW_N:], p1["rt"] + p1["rbx"][:, RW_N:]).astype(BF16)
        y0 = lanes(p0["rbx"][:, :RW_N] + p0["rkv"], p1["rbx"][:, :RW_N] + p1["rkv"])
        if d == 0:
            y0_fwd = y0
        else:
            y0_s[rows, :] = y0_fwd + y0
        phi_s[d, rows2, :] = bdiag(p0["bx"][:, RW_N:], p1["bx"][:, RW_N:]).astype(BF16)
        psi_s[d, rows2, :] = bdiag(p0["bx"][:, :RW_N] + p0["kv"], p1["bx"][:, :RW_N] + p1["kv"])


def _rwkv_step(i, states, phi_s, psi_s, dec_s, st_s, n_chunks, n_ctx):
    hp = 2 * RW_N
    loads = []
    for d in range(2):
        c = i if d == 0 else _chunk_bwd(i, n_ctx, n_chunks)
        rows2 = pl.ds(pl.multiple_of(c * hp, hp), hp)
        loads.append((rows2, phi_s[d, rows2, :], psi_s[d, rows2, :], dec_s[d, rows2, :]))
    new_states = []
    for d in range(2):
        rows2, phi, psi, dec = loads[d]
        hb = states[d].astype(BF16)
        st_s[d, rows2, :] = hb
        new_states.append(states[d] * dec + _dot(phi, hb) + psi)
    return tuple(new_states)


def _rwkv_output(it, gam_s, y0_s, st_s, bon_s, gate_s, ln_ref, o_ref):
    hp = 2 * RW_N
    items = []
    for cc in range(RW_GC):
        c = it * RW_GC + cc
        rows = pl.ds(pl.multiple_of(c * CHUNK, CHUNK), CHUNK)
        rows2 = pl.ds(pl.multiple_of(c * hp, hp), hp)
        items.append(dict(rows=rows, gam=[gam_s[d, rows, :] for d in range(2)],
                          st=[st_s[d, rows2, :] for d in range(2)], y0=y0_s[rows, :],
                          bon=bon_s[rows, :], gate=gate_s[rows, :]))
    for x in items:
        x["o"] = _dot(x["gam"][0], x["st"][0]) + _dot(x["gam"][1], x["st"][1]) + x["y0"]
    for x in items:
        x["ms"] = _seg_sum(x["o"] * x["o"]) * (1.0 / RW_N)
    for x in items:
        y = x["o"] * lax.rsqrt(x["ms"] + EPS) * ln_ref[...] + x["bon"]
        o_ref[0, x["rows"], :] = y * x["gate"]


def _rwkv_kernel(r_ref, k_ref, v_ref, low_ref, mur_ref, muk_ref, muv_ref, mul_ref, w0_ref, w2_ref, a0_ref,
                 a2_ref, g2_ref, kk_ref, ka_ref, rk_ref, ln_ref, o_ref,
                 r_s, v_s, a_s, bon_s, gate_s, lw_s, kd_s, bb_s, gam_s, y0_s, phi_s, psi_s, dec_s, st_s,
                 *, n_chunks, n_ctx):
    t_len = r_s.shape[0]
    t = lax.broadcasted_iota(jnp.int32, (t_len, 1), 0)
    lo = jnp.where(t < CTX_LEN, 0, CTX_LEN)
    hi = jnp.where(t < CTX_LEN, CTX_LEN, t_len)

    def shift_mix(x, mu):
        prev = jnp.where(t - 1 >= lo, pltpu.roll(x, shift=1, axis=0), 0.0)
        nxt = jnp.where(t + 1 < hi, pltpu.roll(x, shift=t_len - 1, axis=0), 0.0)
        return x + mu * (0.5 * (prev + nxt) - x)

    r = shift_mix(r_ref[0], mur_ref[...])
    k = shift_mix(k_ref[0], muk_ref[...])
    v = shift_mix(v_ref[0], muv_ref[...])
    low = shift_mix(low_ref[0], mul_ref[...])
    kk = k * kk_ref[...]
    kk = kk / jnp.maximum(jnp.sqrt(_seg_sum(kk * kk)), 1e-12)
    r_s[...] = r
    v_s[...] = v
    a_s[...] = -kk
    bonus = jnp.zeros_like(r)
    for d in range(2):
        wl = jnp.tanh(low[:, d * RW_DECAY_RANK:(d + 1) * RW_DECAY_RANK])
        al = low[:, 2 * RW_DECAY_RANK + d * RW_A_RANK:2 * RW_DECAY_RANK + (d + 1) * RW_A_RANK]
        wz = w0_ref[d:d + 1, :] + _dot(wl.astype(BF16), w2_ref[d])
        lw_s[d] = -jnp.exp(-_softplus(-wz) - 0.5)
        a = jax.nn.sigmoid(a0_ref[d:d + 1, :] + _dot(al.astype(BF16), a2_ref[d]))
        kd = k * (1.0 + (a - 1.0) * ka_ref[...])
        kd_s[d] = kd
        bb_s[d] = kk * a
        bonus = bonus + _seg_sum(r * kd * rk_ref[...])
    bon_s[...] = bonus * v
    gl = jax.nn.sigmoid(low[:, 2 * RW_DECAY_RANK + 2 * RW_A_RANK:])
    gate_s[...] = _dot(gl.astype(BF16), g2_ref[...])

    def precompute(it, carry):
        _rwkv_precompute(it, r_s, v_s, a_s, lw_s, kd_s, bb_s, gam_s, y0_s, phi_s, psi_s, dec_s)
        return carry

    lax.fori_loop(0, n_chunks // RW_GC, precompute, 0)
    zero = jnp.zeros((2 * RW_N, 2 * RW_N), F32)
    lax.fori_loop(0, n_chunks, lambda i, st: _rwkv_step(i, st, phi_s, psi_s, dec_s, st_s, n_chunks, n_ctx),
                  (zero, zero))

    def output(it, carry):
        _rwkv_output(it, gam_s, y0_s, st_s, bon_s, gate_s, ln_ref, o_ref)
        return carry

    lax.fori_loop(0, n_chunks // RW_GC, output, 0)


def _rwkv(rkv, low, mu, w0, w2, a0, a2, g2, kk, ka, rk, ln):
    bsz, t, _ = rkv.shape
    hp = 2 * RW_N
    n_hp = RW_W // hp
    n_low = low.shape[2]
    tok = lambda off: pl.BlockSpec((1, t, hp), lambda b, j: (b, 0, off * n_hp + j))
    vec = lambda off: pl.BlockSpec((1, hp), lambda b, j: (0, off * n_hp + j))
    par2 = pl.BlockSpec((2, hp), lambda b, j: (0, j))
    par3 = pl.BlockSpec((2, RW_DECAY_RANK, hp), lambda b, j: (0, 0, j))
    tbuf = lambda: pltpu.VMEM((t, hp), F32)
    dbuf = lambda: pltpu.VMEM((2, t, hp), F32)
    mu_rkv, mu_low = mu[None, :3 * RW_W], mu[None, 3 * RW_W:]
    row = lambda a: a.reshape(1, RW_W)
    return pl.pallas_call(
        functools.partial(_rwkv_kernel, n_chunks=t // CHUNK, n_ctx=CTX_LEN // CHUNK),
        grid=(bsz, n_hp),
        in_specs=[tok(0), tok(1), tok(2), pl.BlockSpec((1, t, n_low), lambda b, j: (b, 0, 0)),
                  vec(0), vec(1), vec(2), pl.BlockSpec((1, n_low), lambda b, j: (0, 0)),
                  par2, par3, par2, par3, pl.BlockSpec((RW_GATE_RANK, hp), lambda b, j: (0, j)),
                  vec(0), vec(0), vec(0), vec(0)],
        out_specs=pl.BlockSpec((1, t, hp), lambda b, j: (b, 0, j)),
        out_shape=jax.ShapeDtypeStruct((bsz, t, RW_W), F32),
        scratch_shapes=[tbuf(), tbuf(), tbuf(), tbuf(), tbuf(), dbuf(), dbuf(), dbuf(),
                        pltpu.VMEM((2, t, hp), BF16), tbuf(), pltpu.VMEM((2, 2 * t, hp), BF16), pltpu.VMEM((2, 2 * t, hp), F32),
                        pltpu.VMEM((2, 2 * t, hp), F32), pltpu.VMEM((2, 2 * t, hp), BF16)],
        compiler_params=_cparams("arbitrary", "arbitrary"),
        name="rwkv7",
    )(rkv, rkv, rkv, low, mu_rkv, mu_rkv, mu_rkv, mu_low, w0, w2.astype(BF16), a0, a2.astype(BF16),
      g2.astype(BF16), row(kk), row(ka), row(rk), row(ln))


def _cd_out_kernel(att_ref, rw_ref, h_ref, w_ref, g1_ref, a2_ref, b2_ref, h_out, v_out):
    n_att = MLA_HEADS * MLA_V
    y = _dot(att_ref[...], w_ref[:n_att, :]) + _dot(rw_ref[...].astype(BF16), w_ref[n_att:, :])
    _residual_and_next(y, h_ref, g1_ref, a2_ref, b2_ref, h_out, v_out)


def _cd_out(att, rw, h, w_out, g1, a2, b2, nb):
    rows = att.shape[0]
    tm = TOK_BLK
    lat = lambda: pl.BlockSpec((tm, D_MODEL), lambda i: (_lat_block(i, nb), 0))
    return pl.pallas_call(
        _cd_out_kernel,
        grid=(rows // tm,),
        in_specs=[_row_spec(tm, att.shape[1]), lat(), lat(), _full_spec(w_out.shape), _tab_spec(), _tab_spec(),
                  _tab_spec()],
        out_specs=[_row_spec(tm, D_MODEL), _row_spec(tm, D_MODEL)],
        out_shape=[jax.ShapeDtypeStruct((rows, D_MODEL), F32)] * 2,
        compiler_params=_cparams("arbitrary"),
        name="cd_out",
    )(att, rw, h, w_out, g1, a2, b2)


def _pad_cols(w, width):
    return jnp.pad(w, ((0, 0), (0, width - w.shape[1])))


def _split_cols(w, sizes):
    assert sum(sizes) == w.shape[1]
    out, start = [], 0
    for s in sizes:
        out.append(w[:, start:start + s])
        start += s
    return out


def _layer_ab(h, tabs, ab_w_in, ab_w_out, ml_i_bias, ml_f_bias, ml_norm, ssd_conv_w, ssd_conv_b, ssd_dt_bias,
              ssd_a_log, ssd_d, ssd_norm, bsz):
    a1, b1, g1, a2, b2, _ = tabs
    t = h.shape[0] // bsz
    q, k, v, og, ig, fg, z, xbc, dt = _split_cols(
        ab_w_in, (ML_QKW, ML_QKW, ML_W, ML_W, 2 * ML_HEADS, 2 * ML_HEADS, SSD_W, SSD_XBC, 2 * SSD_HEADS))
    small = _pad_cols(jnp.concatenate([ig, fg, dt], axis=1), LANES)
    w_in = jnp.concatenate([q, k, v, og, z, xbc, small], axis=1).astype(BF16)
    widths = (2 * ML_QKW + ML_W, ML_W, SSD_W, SSD_XBC, LANES)
    starts = [0]
    for wd in widths[:-1]:
        starts.append(starts[-1] + wd)
    qkv, og_p, z_p, xbc_p, small_p = _norm_proj(h, a1, b1, w_in, tuple(zip(starts, widths)),
                                               (BF16, F32, F32, F32, F32), "ab_in_proj")
    pad = LANES - 4 * ML_HEADS
    bias_row = jnp.concatenate([ml_i_bias.reshape(-1), ml_f_bias.reshape(-1), jnp.zeros((pad,), F32)])[None]
    hf, hb = _mlstm(qkv.reshape(bsz, t, -1), small_p.reshape(bsz, t, LANES), bias_row)
    xbc_c = _ssd_conv(xbc_p.reshape(bsz, t, SSD_XBC), ssd_conv_w, ssd_conv_b)
    lead = jnp.zeros((4 * ML_HEADS,), F32)
    tail = jnp.zeros((LANES - 4 * ML_HEADS - 2 * SSD_HEADS,), F32)
    dtb_row = jnp.concatenate([lead, ssd_dt_bias.reshape(-1), tail])[None]
    ea_row = jnp.concatenate([lead, jnp.exp(ssd_a_log).reshape(-1), tail])[None]
    yf, yb = _ssd(xbc_c, small_p.reshape(bsz, t, LANES), dtb_row, ea_row)
    flat = lambda a: a.reshape(bsz * t, -1)
    return _ab_out(flat(hf), flat(hb), og_p, flat(yf), flat(yb), flat(xbc_c), z_p, h, ml_norm[None],
                   jnp.repeat(ssd_d, SSD_P)[None], ssd_norm[None], ab_w_out.astype(BF16), g1, a2, b2)


def _pad_heads(w, n_heads, take, width):
    k, n = w.shape
    w3 = w.reshape(k, n_heads, n // n_heads)[:, :, take]
    return jnp.pad(w3, ((0, 0), (0, 0), (0, width - w3.shape[2]))).reshape(k, n_heads * width)


def _layer_cd(h, tabs_all, tabs_lat, cd_w_in, cd_w_out, mla_qn, mla_w_uq, mla_kvn, mla_w_ukv, rw_mu, rw_w0, rw_w2,
              rw_a0, rw_a2, rw_g2, rw_kk, rw_ka, rw_rk, rw_ln, bsz):
    a1, b1 = tabs_all[0], tabs_all[1]
    _, _, g1, a2, b2, _ = tabs_lat
    t = h.shape[0] // bsz
    n_lat = MLA_Q_RANK + MLA_KV_RANK
    n_mla = n_lat + MLA_ROPE
    n_low = 2 * RW_DECAY_RANK + 2 * RW_A_RANK + RW_GATE_RANK
    mla_w = n_lat + MLA_KPAD
    k_rope = jnp.pad(cd_w_in[:, n_lat:n_mla], ((0, 0), (MLA_NOPE, MLA_KPAD - MLA_NOPE - MLA_ROPE)))
    w_in = jnp.concatenate([cd_w_in[:, :n_lat], k_rope, cd_w_in[:, n_mla:]], axis=1).astype(BF16)
    segs = ((0, mla_w), (mla_w, 3 * RW_W), (mla_w + 3 * RW_W, n_low))
    m, rkv, low = _norm_proj(h, a1, b1, w_in, segs, (F32, F32, F32), "cd_in_proj")
    wq = _pad_heads(mla_w_uq, MLA_HEADS, slice(None), MLA_KPAD).astype(BF16)
    wk = _pad_heads(mla_w_ukv, MLA_HEADS, slice(0, MLA_NOPE), MLA_KPAD).astype(BF16)
    wv = _pad_heads(mla_w_ukv, MLA_HEADS, slice(MLA_NOPE, None), MLA_V).astype(BF16)
    qt, k, vt = _mla_prep(m, mla_qn[None], mla_kvn[None], wq, wk, wv, bsz)
    per_b = lambda a: a.reshape(bsz, t, -1)
    att = _attention(qt, per_b(k), vt, bsz)
    rw = _rwkv(per_b(rkv), per_b(low), rw_mu, rw_w0, rw_w2, rw_a0, rw_a2, rw_g2, rw_kk, rw_ka, rw_rk, rw_ln)
    return _cd_out(att, rw.reshape(bsz * t, RW_W), h, cd_w_out.astype(BF16), g1, a2, b2, t // TOK_BLK)


def _moe_weights(router_w, router_bias, exp_w_gate, exp_w_up, exp_w_down, sh_w_gate, sh_w_up, sh_w_down):
    stack = lambda s, e: jnp.concatenate([s[None], e], axis=0).astype(BF16)
    return (router_w.T, router_bias[:, None], stack(sh_w_gate, exp_w_gate), stack(sh_w_up, exp_w_up),
            stack(sh_w_down, exp_w_down))


def kernel(x, c, ctx, c_ctx, w_mod, b_mod, norm_g, ab_w_in, ab_w_out, ml_i_bias, ml_f_bias, ml_norm, ssd_conv_w, ssd_conv_b, ssd_dt_bias, ssd_a_log, ssd_d, ssd_norm, cd_w_in, cd_w_out, mla_qn, mla_w_uq, mla_kvn, mla_w_ukv, rw_mu, rw_w0, rw_w2, rw_a0, rw_a2, rw_g2, rw_kk, rw_ka, rw_rk, rw_ln, router_w, router_bias, exp_w_gate, exp_w_up, exp_w_down, sh_w_gate, sh_w_up, sh_w_down):
    bsz, seq, _ = x.shape
    t = CTX_LEN + seq
    h = jnp.concatenate([ctx, x], axis=1).reshape(bsz * t, D_MODEL)

    mod = _modulation(c, c_ctx, w_mod[0], b_mod[0])
    tabs = _block_tables(mod, norm_g[0], bsz, t // TOK_BLK, True)
    h, v = _layer_ab(h, tabs, ab_w_in[0], ab_w_out[0], ml_i_bias[0], ml_f_bias[0], ml_norm[0], ssd_conv_w[0],
                     ssd_conv_b[0], ssd_dt_bias[0], ssd_a_log[0], ssd_d[0], ssd_norm[0], bsz)
    moe_w = _moe_weights(router_w, router_bias, exp_w_gate[0], exp_w_up[0], exp_w_down[0], sh_w_gate[0],
                         sh_w_up[0], sh_w_down[0])
    h = _moe(v, h, *moe_w, tabs[5])

    mod = _modulation(c, c_ctx, w_mod[1], b_mod[1])
    nb = t // TOK_BLK
    tabs_all = _block_tables(mod, norm_g[1], bsz, nb, True)
    tabs_lat = _block_tables(mod, norm_g[1], bsz, nb - 1, False)
    h, v = _layer_cd(h, tabs_all, tabs_lat, cd_w_in[0], cd_w_out[0], mla_qn[0], mla_w_uq[0], mla_kvn[0],
                     mla_w_ukv[0], rw_mu[0], rw_w0[0], rw_w2[0], rw_a0[0], rw_a2[0], rw_g2[0], rw_kk[0], rw_ka[0],
                     rw_rk[0], rw_ln[0], bsz)
    moe_w = _moe_weights(router_w, router_bias, exp_w_gate[1], exp_w_up[1], exp_w_down[1], sh_w_gate[1],
                         sh_w_up[1], sh_w_down[1])
    h = _moe(v, h, *moe_w, tabs_lat[5])
    return h.reshape(bsz, seq, D_MODEL)
```

```python
import functools

import jax
import jax.numpy as jnp
from jax import lax
from jax.experimental import pallas as pl
from jax.experimental.pallas import tpu as pltpu

F32 = jnp.float32
BF16 = jnp.bfloat16
HIGHEST = lax.Precision.HIGHEST

D_MODEL = 1024
CTX_LEN = 256
GRID_W = 64
EPS = 1e-6
ROPE_BASE = 10000.0
TOK_BLK = 256
CHUNK = 64
SCAN_ROWS = 4
LANES = 128

ML_HEADS, ML_QK, ML_V = 4, 128, 256
ML_QKW, ML_W = ML_HEADS * ML_QK, ML_HEADS * ML_V
SSD_HEADS, SSD_P, SSD_N, SSD_GROUPS, SSD_CONV = 16, 64, 64, 2, 5
SSD_HPG = SSD_HEADS // SSD_GROUPS
SSD_W, SSD_GN = SSD_HEADS * SSD_P, SSD_GROUPS * SSD_N
SSD_XBC = SSD_W + 2 * SSD_GN
MLA_HEADS, MLA_NOPE, MLA_ROPE, MLA_V = 8, 64, 32, 64
MLA_Q_RANK, MLA_KV_RANK = 384, 256
MLA_SCALE = (MLA_NOPE + MLA_ROPE) ** -0.5
RW_HEADS, RW_N = 16, 64
RW_W = RW_HEADS * RW_N
RW_DECAY_RANK, RW_A_RANK, RW_GATE_RANK = 64, 64, 128
N_EXPERTS, N_GROUPS, EXPERT_FF = 16, 4, 512
EXPERTS_PER_GROUP = N_EXPERTS // N_GROUPS

VMEM_LIMIT = 56 * 1024 * 1024


def _cparams(*sem):
    return pltpu.CompilerParams(dimension_semantics=sem, vmem_limit_bytes=VMEM_LIMIT)


def _silu(x):
    return x * jax.nn.sigmoid(x)


def _softplus(x):
    return jnp.maximum(x, 0.0) + jnp.log1p(jnp.exp(-jnp.abs(x)))


def _log_sigmoid(x):
    return jnp.minimum(x, 0.0) - jnp.log1p(jnp.exp(-jnp.abs(x)))


def _rms(x):
    return x * lax.rsqrt(jnp.mean(x * x, axis=-1, keepdims=True) + EPS)


def _dot_t(a, b):
    return lax.dot_general(a, b, (((1,), (1,)), ((), ())), preferred_element_type=F32)


def _dot(a, b):
    return jnp.dot(a, b, preferred_element_type=F32)


def _split3(x):
    hi = x.astype(BF16)
    r1 = x - hi.astype(F32)
    mid = r1.astype(BF16)
    return hi, mid, (r1 - mid.astype(F32)).astype(BF16)


def _dot_01(m, x, terms=3):
    parts = _split3(x)[:terms]
    return sum(_dot(m, p) for p in parts)


def _dot_x01(x, m, terms=3):
    parts = _split3(x)[:terms]
    return sum(_dot(p, m) for p in parts)


def _tri_masks(d):
    row = lax.broadcasted_iota(jnp.int32, (CHUNK, CHUNK), 0)
    col = lax.broadcasted_iota(jnp.int32, (CHUNK, CHUNK), 1)
    incl = (col <= row) if d == 0 else (col >= row)
    strict = (col < row) if d == 0 else (col > row)
    return incl, strict


def _mod_kernel(s_ref, w_ref, b_ref, o_ref):
    s = _silu(s_ref[...])
    o_ref[...] = jnp.dot(s, w_ref[...], precision=HIGHEST, preferred_element_type=F32) + b_ref[...]


def _modulation(c, c_ctx, w, b):
    bsz = c.shape[0]
    rows = 8 * ((bsz + 1 + 7) // 8)
    s = jnp.zeros((rows, D_MODEL), F32).at[:bsz].set(c).at[bsz].set(c_ctx)
    n = w.shape[1]
    return pl.pallas_call(
        _mod_kernel,
        grid=(n // D_MODEL,),
        in_specs=[pl.BlockSpec((rows, D_MODEL), lambda j: (0, 0)),
                  pl.BlockSpec((D_MODEL, D_MODEL), lambda j: (0, j)),
                  pl.BlockSpec((1, D_MODEL), lambda j: (0, j))],
        out_specs=pl.BlockSpec((rows, D_MODEL), lambda j: (0, j)),
        out_shape=jax.ShapeDtypeStruct((rows, n), F32),
        compiler_params=_cparams("arbitrary"),
        name="modulation",
    )(s, w, b.reshape(1, n))


def _block_tables(mod, norm_g, bsz, blocks_per_row, with_ctx):
    j = jnp.arange(blocks_per_row)
    b = jnp.arange(bsz)
    if with_ctx:
        sel = jnp.where(j[None, :] == 0, bsz, b[:, None]).reshape(-1)
    else:
        sel = jnp.broadcast_to(b[:, None], (bsz, blocks_per_row)).reshape(-1)
    m = mod[sel]
    sh_a, sc_a, g_a, sh_f, sc_f, g_f = jnp.split(m, 6, axis=-1)
    tabs = (norm_g[0] * (1 + sc_a), sh_a, g_a * norm_g[1], norm_g[2] * (1 + sc_f), sh_f, g_f * norm_g[3])
    return tuple(t[:, None, :] for t in tabs)


def _norm_proj_kernel(x_ref, a_ref, b_ref, w_ref, *refs, segs, nsub):
    out_refs, u_ref = refs[:len(segs)], refs[len(segs)]
    for s in range(nsub):
        rows = slice(s * TOK_BLK, (s + 1) * TOK_BLK)
        u_ref[rows, :] = (_rms(x_ref[rows, :]) * a_ref[s] + b_ref[s]).astype(BF16)
    u = u_ref[...]
    for o_ref, (start, width) in zip(out_refs, segs):
        o_ref[...] = _dot(u, w_ref[:, start:start + width]).astype(o_ref.dtype)


def _norm_proj(x, a_tab, b_tab, w, segs, dtypes, name):
    rows = x.shape[0]
    nsub = 2
    tm = nsub * TOK_BLK
    assert rows % tm == 0
    n = w.shape[1]
    return pl.pallas_call(
        functools.partial(_norm_proj_kernel, segs=segs, nsub=nsub),
        grid=(rows // tm,),
        in_specs=[pl.BlockSpec((tm, D_MODEL), lambda i: (i, 0)),
                  pl.BlockSpec((nsub, 1, D_MODEL), lambda i: (i, 0, 0)),
                  pl.BlockSpec((nsub, 1, D_MODEL), lambda i: (i, 0, 0)),
                  pl.BlockSpec((D_MODEL, n), lambda i: (0, 0))],
        out_specs=[pl.BlockSpec((tm, wd), lambda i: (i, 0)) for _, wd in segs],
        out_shape=[jax.ShapeDtypeStruct((rows, wd), dt) for (_, wd), dt in zip(segs, dtypes)],
        scratch_shapes=[pltpu.VMEM((tm, D_MODEL), BF16)],
        compiler_params=_cparams("arbitrary"),
        name=name,
    )(x, a_tab, b_tab, w)


def _chunk_fwd(i):
    return i


def _chunk_bwd(i, n_ctx, n_all):
    return jnp.where(i < n_ctx, n_ctx - 1 - i, n_all + n_ctx - 1 - i)


def _mlstm_chunk(gates, refs, c_old, n_old, m_old, bias_ref, seli_ref, selv_ref, selq_ref, selit_ref, nmask_ref):
    scale = ML_QK ** -0.5
    nch = 2 * ML_HEADS
    lane = lax.broadcasted_iota(jnp.int32, (1, LANES), 1)
    fwd_cols, valid = lane < ML_HEADS, lane < nch
    pre, bcum = [], []
    for d in range(2):
        p_d = gates[d] + bias_ref[...]
        pre.append(p_d)
        bcum.append(_dot_01(_tri_masks(d)[0].astype(BF16), _log_sigmoid(p_d)))
    ig = jnp.where(valid, jnp.where(fwd_cols, pre[0], pre[1]), 0.0)
    b_raw = jnp.where(lane < nch + ML_HEADS, bcum[0], bcum[1])
    b_al = jnp.where(valid, pltpu.roll(b_raw, shift=LANES - nch, axis=1), 0.0)
    c_cols = ig - b_al
    row = lax.broadcasted_iota(jnp.int32, (CHUNK, LANES), 0)
    pmax, smax, sh = c_cols, c_cols, 1
    while sh < CHUNK:
        pmax = jnp.maximum(pmax, jnp.where(row >= sh, pltpu.roll(pmax, shift=sh, axis=0), -jnp.inf))
        smax = jnp.maximum(smax, jnp.where(row < CHUNK - sh, pltpu.roll(smax, shift=CHUNK - sh, axis=0), -jnp.inf))
        sh *= 2
    cmax = jnp.where(fwd_cols, pmax, smax)
    ends = lambda a: jnp.where(fwd_cols, a[CHUNK - 1:CHUNK, :], a[0:1, :])
    m_j = b_al + jnp.maximum(m_old, cmax)
    w_int = jnp.exp(b_al + m_old - m_j)
    btot = ends(b_al)
    m_new = btot + jnp.maximum(m_old, ends(cmax))
    e_cols = jnp.exp(btot + c_cols - m_new)
    dec = jnp.exp(btot + m_old - m_new)

    seli, selv = seli_ref[...], selv_ref[...]
    lane5 = lax.broadcasted_iota(jnp.int32, (CHUNK, nch * CHUNK), 1)
    row5 = lax.broadcasted_iota(jnp.int32, (CHUNK, nch * CHUNK), 0)
    ahead = lane5 % CHUNK - row5
    causal = jnp.where(lane5 < ML_HEADS * CHUNK, ahead, -ahead) <= 0
    spread = _dot_x01(jnp.concatenate([b_al, ig, m_j], axis=0), seli, terms=2)
    b_x, ig_x, mj_x = spread[:CHUNK], spread[CHUNK:2 * CHUNK], spread[2 * CHUNK:]
    c_row = _dot_01(jnp.ones((CHUNK, CHUNK), BF16), jnp.where(ahead == 0, ig_x - b_x, 0.0),
                    terms=2)
    p = jnp.exp(jnp.where(causal, b_x + c_row - mj_x, -jnp.inf))

    q = [refs[ci // ML_HEADS][:, (ci % ML_HEADS) * ML_QK:(ci % ML_HEADS + 1) * ML_QK] for ci in range(nch)]
    k = [refs[ci // ML_HEADS][:, ML_QKW + (ci % ML_HEADS) * ML_QK:ML_QKW + (ci % ML_HEADS + 1) * ML_QK]
         for ci in range(nch)]
    v = [refs[ci // ML_HEADS][:, 2 * ML_QKW + (ci % ML_HEADS) * ML_V:2 * ML_QKW + (ci % ML_HEADS + 1) * ML_V]
         for ci in range(nch)]
    s_all = jnp.concatenate([_dot_t(q[ci], k[ci]) for ci in range(nch)], axis=1) * scale * p
    nq = sum(_dot(refs[d][:, :ML_QKW], n_old[d].astype(BF16)) for d in range(2))
    den = _dot_x01(s_all, selit_ref[...]) + w_int * nq * scale
    inv = 1.0 / jnp.maximum(jnp.abs(den), jnp.exp(-m_j))
    s_b = (s_all * _dot_x01(inv, seli, terms=2)).astype(BF16)
    spread = _dot_x01(jnp.concatenate([w_int * inv * scale, e_cols], axis=0), selq_ref[...], terms=2)
    dec_x = _dot_x01(jnp.broadcast_to(dec, (8, LANES)), selv, terms=2)[0:1, :]
    ke_t = []
    for d in range(2):
        cols = slice(d * ML_QKW, (d + 1) * ML_QKW)
        ke_t.append((refs[d][:, ML_QKW:2 * ML_QKW].astype(F32) * spread[CHUNK:, cols]).T.astype(BF16))
    outs, c_new = [], []
    for ci in range(nch):
        d, h = divmod(ci, ML_HEADS)
        qw = (q[ci].astype(F32) * spread[:CHUNK, ci * ML_QK:(ci + 1) * ML_QK]).astype(BF16)
        outs.append(_dot(s_b[:, ci * CHUNK:(ci + 1) * CHUNK], v[ci]) + _dot(qw, c_old[ci].astype(BF16)))
        c_new.append(dec_x[:, ci * ML_V:(ci + 1) * ML_V] * c_old[ci]
                     + _dot(ke_t[d][h * ML_QK:(h + 1) * ML_QK, :], v[ci]))
    ones = jnp.ones((CHUNK, LANES), BF16)
    n_new = [dec * n_old[d] + jnp.where(nmask_ref[d] > 0, _dot(ke_t[d], ones), 0.0) for d in range(2)]
    return (jnp.concatenate(outs[:ML_HEADS], axis=1), jnp.concatenate(outs[ML_HEADS:], axis=1)), c_new, n_new, m_new


def _mlstm_kernel(qkv_f, g_f, qkv_b, g_b, bias_ref, seli_ref, selv_ref, selq_ref, selit_ref, nmask_ref, hf_ref, hb_ref,
                  c_ref, n_ref, m_ref):
    @pl.when(pl.program_id(1) == 0)
    def _():
        c_ref[...] = jnp.zeros_like(c_ref)
        n_ref[...] = jnp.zeros_like(n_ref)
        m_ref[...] = jnp.zeros_like(m_ref)

    nch = 2 * ML_HEADS
    loaded = [((g_f[bb], g_b[bb]), (qkv_f[bb], qkv_b[bb]), [c_ref[bb * nch + ci] for ci in range(nch)],
               [n_ref[2 * bb + d] for d in range(2)], m_ref[8 * bb:8 * bb + 1, :]) for bb in range(SCAN_ROWS)]
    done = [_mlstm_chunk(*args, bias_ref, seli_ref, selv_ref, selq_ref, selit_ref, nmask_ref) for args in loaded]
    for bb, ((hf, hb), c_new, n_new, m_new) in enumerate(done):
        hf_ref[bb] = hf.astype(hf_ref.dtype)
        hb_ref[bb] = hb.astype(hb_ref.dtype)
        for ci in range(nch):
            c_ref[bb * nch + ci] = c_new[ci]
        for d in range(2):
            n_ref[2 * bb + d] = n_new[d]
        m_ref[8 * bb:8 * bb + 8, :] = jnp.broadcast_to(m_new, (8, LANES))


def _mlstm(qkv, gates, bias_row):
    bsz, t, _ = qkv.shape
    nc, nctx = t // CHUNK, CTX_LEN // CHUNK
    bwd = functools.partial(_chunk_bwd, n_ctx=nctx, n_all=nc)
    w = qkv.shape[2]
    sel = _mlstm_selectors()
    nr = SCAN_ROWS
    assert bsz % nr == 0
    return pl.pallas_call(
        _mlstm_kernel,
        grid=(bsz // nr, nc),
        in_specs=[pl.BlockSpec((nr, CHUNK, w), lambda b, i: (b, i, 0)),
                  pl.BlockSpec((nr, CHUNK, LANES), lambda b, i: (b, i, 0)),
                  pl.BlockSpec((nr, CHUNK, w), lambda b, i: (b, bwd(i), 0)),
                  pl.BlockSpec((nr, CHUNK, LANES), lambda b, i: (b, bwd(i), 0)),
                  pl.BlockSpec((1, LANES), lambda b, i: (0, 0))] + [
                      pl.BlockSpec(a.shape, lambda b, i, n=a.ndim: (0,) * n) for a in sel],
        out_specs=[pl.BlockSpec((nr, CHUNK, ML_W), lambda b, i: (b, i, 0)),
                   pl.BlockSpec((nr, CHUNK, ML_W), lambda b, i: (b, bwd(i), 0))],
        out_shape=[jax.ShapeDtypeStruct((bsz, t, ML_W), BF16)] * 2,
        scratch_shapes=[pltpu.VMEM((nr * 2 * ML_HEADS, ML_QK, ML_V), F32),
                        pltpu.VMEM((nr * 2, ML_QKW, LANES), F32),
                        pltpu.VMEM((nr * 8, LANES), F32)],
        compiler_params=_cparams("arbitrary", "arbitrary"),
        name="mlstm_scan",
    )(qkv, gates, qkv, gates, bias_row, *sel)


def _mlstm_selectors():
    nch = 2 * ML_HEADS
    col = jnp.arange(LANES)[:, None]
    seli = (col == jnp.arange(nch * CHUNK)[None, :] // CHUNK).astype(BF16)
    selv = (col == jnp.arange(nch * ML_V)[None, :] // ML_V).astype(BF16)
    head = (jnp.arange(ML_QKW) // ML_QK)[None, :, None]
    chain = jnp.arange(2)[:, None, None] * ML_HEADS + head
    nmask = (jnp.arange(LANES)[None, None, :] == chain).astype(F32)
    selq = (col == jnp.arange(nch * ML_QK)[None, :] // ML_QK).astype(BF16)
    return seli, selv, selq, seli.T, nmask


def _conv_kernel(x_ref, w_ref, b_ref, o_ref):
    x = x_ref[0]
    t_len = x.shape[0]
    t = lax.broadcasted_iota(jnp.int32, (t_len, 1), 0)
    lo = jnp.where(t < CTX_LEN, 0, CTX_LEN)
    hi = jnp.where(t < CTX_LEN, CTX_LEN, t_len)
    half = SSD_CONV // 2
    acc = x * w_ref[half:half + 1, :]
    for tap in range(SSD_CONV):
        d = tap - half
        if d == 0:
            continue
        shifted = pltpu.roll(x, shift=(-d) % t_len, axis=0)
        ok = (t + d >= lo) & (t + d < hi)
        acc = acc + jnp.where(ok, shifted, 0.0) * w_ref[tap:tap + 1, :]
    o_ref[0] = _silu(acc + b_ref[...])


def _ssd_conv(xbc, w, b):
    bsz, t, n = xbc.shape
    tn = 256
    return pl.pallas_call(
        _conv_kernel,
        grid=(bsz, n // tn),
        in_specs=[pl.BlockSpec((1, t, tn), lambda b, j: (b, 0, j)),
                  pl.BlockSpec((SSD_CONV, tn), lambda b, j: (0, j)),
                  pl.BlockSpec((1, tn), lambda b, j: (0, j))],
        out_specs=pl.BlockSpec((1, t, tn), lambda b, j: (b, 0, j)),
        out_shape=jax.ShapeDtypeStruct((bsz, t, n), F32),
        compiler_params=_cparams("arbitrary", "arbitrary"),
        name="ssd_conv",
    )(xbc, w, b.reshape(1, n))


def _ssd_kernel(x_f, g_f, x_b, g_b, dtb_ref, ea_ref, sel_ref, yf_ref, yb_ref, st_ref):
    @pl.when(pl.program_id(1) == 0)
    def _():
        st_ref[...] = jnp.zeros_like(st_ref)

    lane = lax.broadcasted_iota(jnp.int32, (CHUNK, SSD_W), 1)
    row = lax.broadcasted_iota(jnp.int32, (CHUNK, SSD_W), 0)
    s_idx = lane % SSD_P
    diag = s_idx == row
    ones = jnp.ones((CHUNK, CHUNK), BF16)
    half = lax.broadcasted_iota(jnp.int32, (CHUNK, 2 * SSD_P), 1) < SSD_P
    loads = []
    for bb in range(SCAN_ROWS):
        for d, (x_ref, g_ref, o_ref) in enumerate(((x_f, g_f, yf_ref), (x_b, g_b, yb_ref))):
            loads.append((bb, d, x_ref[bb, :, :SSD_W], x_ref[bb, :, SSD_W:SSD_W + SSD_GN],
                          x_ref[bb, :, SSD_W + SSD_GN:].astype(BF16), g_ref[bb], st_ref[2 * bb + d], o_ref))
    outs = []
    for bb, d, xs, bm_f, cm, gates, st_old, o_ref in loads:
        bm = bm_f.astype(BF16)
        incl, _ = _tri_masks(d)
        mask_t = (s_idx <= row) if d == 0 else (s_idx >= row)
        last = CHUNK - 1 if d == 0 else 0
        dtv = _softplus(gates + dtb_ref[...])
        bcum = _dot_01(incl.astype(BF16), -dtv * ea_ref[...])
        b_col = _dot_x01(bcum, sel_ref[d], terms=2)
        dt_col = _dot_x01(dtv, sel_ref[d], terms=2)
        b_row = _dot_01(ones, jnp.where(diag, b_col, 0.0), terms=2)
        dt_row = _dot_01(ones, jnp.where(diag, dt_col, 0.0), terms=2)
        btot = b_col[last:last + 1, :]
        seg = jnp.exp(jnp.where(mask_t, b_col - b_row, -jnp.inf))
        cb = []
        for g in range(SSD_GROUPS):
            cols = slice(g * SSD_N, (g + 1) * SSD_N)
            cbg = _dot_t(cm[:, cols], bm[:, cols])
            cb.extend([jnp.concatenate([cbg, cbg], axis=1)] * (SSD_HPG // 2))
        w = (jnp.concatenate(cb, axis=1) * seg * dt_row).astype(BF16)
        xsb = xs.astype(BF16)
        st_b = st_old.astype(BF16)
        y_parts = []
        for p in range(SSD_HEADS // 2):
            cols = slice(p * 2 * SSD_P, (p + 1) * 2 * SSD_P)
            xp = xsb[:, cols]
            x_bd = jnp.concatenate([jnp.where(half, xp, 0), jnp.where(half, 0, xp)], axis=0)
            y_parts.append(_dot(w[:, cols], x_bd))
        ys_parts = [_dot(cm[:, g * SSD_N:(g + 1) * SSD_N], st_b[:, g * SSD_HPG * SSD_P:(g + 1) * SSD_HPG * SSD_P])
                    for g in range(SSD_GROUPS)]
        y = jnp.concatenate(y_parts, axis=1) + jnp.exp(b_col) * jnp.concatenate(ys_parts, axis=1)
        xe = (xs * (jnp.exp(btot - b_col) * dt_col)).astype(BF16)
        upd = [_dot(bm_f[:, g * SSD_N:(g + 1) * SSD_N].T.astype(BF16),
                    xe[:, g * SSD_HPG * SSD_P:(g + 1) * SSD_HPG * SSD_P]) for g in range(SSD_GROUPS)]
        outs.append((bb, d, o_ref, y, jnp.exp(btot) * st_old + jnp.concatenate(upd, axis=1)))
    for bb, d, o_ref, y, st_new in outs:
        o_ref[bb] = y.astype(o_ref.dtype)
        st_ref[2 * bb + d] = st_new


def _ssd(xbc, gates, dtb_row, ea_row):
    bsz, t, w = xbc.shape
    nc, nctx = t // CHUNK, CTX_LEN // CHUNK
    bwd = functools.partial(_chunk_bwd, n_ctx=nctx, n_all=nc)
    nr = SCAN_ROWS
    assert bsz % nr == 0
    return pl.pallas_call(
        _ssd_kernel,
        grid=(bsz // nr, nc),
        in_specs=[pl.BlockSpec((nr, CHUNK, w), lambda b, i: (b, i, 0)),
                  pl.BlockSpec((nr, CHUNK, LANES), lambda b, i: (b, i, 0)),
                  pl.BlockSpec((nr, CHUNK, w), lambda b, i: (b, bwd(i), 0)),
                  pl.BlockSpec((nr, CHUNK, LANES), lambda b, i: (b, bwd(i), 0)),
                  pl.BlockSpec((1, LANES), lambda b, i: (0, 0)),
                  pl.BlockSpec((1, LANES), lambda b, i: (0, 0)),
                  pl.BlockSpec((2, LANES, SSD_W), lambda b, i: (0, 0, 0))],
        out_specs=[pl.BlockSpec((nr, CHUNK, SSD_W), lambda b, i: (b, i, 0)),
                   pl.BlockSpec((nr, CHUNK, SSD_W), lambda b, i: (b, bwd(i), 0))],
        out_shape=[jax.ShapeDtypeStruct((bsz, t, SSD_W), BF16)] * 2,
        scratch_shapes=[pltpu.VMEM((2 * nr, SSD_N, SSD_W), F32)],
        compiler_params=_cparams("arbitrary", "arbitrary"),
        name="ssd_scan",
    )(xbc, gates, xbc, gates, dtb_row, ea_row, _ssd_head_select())


def _ssd_head_select():
    col = jnp.arange(LANES)[None, :, None]
    head = (jnp.arange(SSD_W) // SSD_P)[None, None, :]
    d = jnp.arange(2)[:, None, None]
    return (col == 4 * ML_HEADS + d * SSD_HEADS + head).astype(BF16)


def _residual_and_next(y, h_ref, g1_ref, a2_ref, b2_ref, h_out, v_out):
    h_new = h_ref[...] + g1_ref[0] * _rms(y)
    h_out[...] = h_new
    v_out[...] = _rms(h_new) * a2_ref[0] + b2_ref[0]


def _ab_out_kernel(hf_ref, hb_ref, og_ref, yf_ref, yb_ref, xs_ref, z_ref, h_ref, mln_ref, sd_ref, sn_ref,
                   w_ref, g1_ref, a2_ref, b2_ref, h_out, v_out):
    y = None
    for hd in range(ML_HEADS):
        cols = slice(hd * ML_V, (hd + 1) * ML_V)
        hsum = hf_ref[:, cols].astype(F32) + hb_ref[:, cols].astype(F32)
        ml = _rms(hsum) * mln_ref[:, cols] * jax.nn.sigmoid(og_ref[:, cols])
        part = _dot(ml.astype(BF16), w_ref[cols, :])
        y = part if y is None else y + part
    ys = (yf_ref[...].astype(F32) + yb_ref[...].astype(F32) + sd_ref[...] * xs_ref[...]) * _silu(z_ref[...])
    y = y + _dot((_rms(ys) * sn_ref[...]).astype(BF16), w_ref[ML_W:, :])
    _residual_and_next(y, h_ref, g1_ref, a2_ref, b2_ref, h_out, v_out)


def _row_spec(tm, width, col=0):
    return pl.BlockSpec((tm, width), lambda i: (i, col))


def _tab_spec():
    return pl.BlockSpec((1, 1, D_MODEL), lambda i: (i, 0, 0))


def _full_spec(shape):
    return pl.BlockSpec(shape, lambda i: tuple(0 for _ in shape))


def _ab_out(hf, hb, og, yf, yb, xbc, z, h, ml_norm, ssd_d_row, ssd_norm, w_out, g1, a2, b2):
    rows = h.shape[0]
    tm = TOK_BLK
    wide = lambda: _row_spec(tm, D_MODEL)
    return pl.pallas_call(
        _ab_out_kernel,
        grid=(rows // tm,),
        in_specs=[wide(), wide(), wide(), wide(), wide(), wide(), wide(), wide(),
                  _full_spec((1, ML_W)), _full_spec((1, SSD_W)), _full_spec((1, SSD_W)),
                  _full_spec(w_out.shape), _tab_spec(), _tab_spec(), _tab_spec()],
        out_specs=[wide(), wide()],
        out_shape=[jax.ShapeDtypeStruct((rows, D_MODEL), F32)] * 2,
        compiler_params=_cparams("arbitrary"),
        name="ab_out",
    )(hf, hb, og, yf, yb, xbc, z, h, ml_norm, ssd_d_row, ssd_norm, w_out, g1, a2, b2)


def _router_gates_t(x, rw_t, rb_col):
    tm = x.shape[0]
    aff = jax.nn.sigmoid(lax.dot_general(rw_t, x, (((1,), (1,)), ((), ())), precision=HIGHEST,
                                         preferred_element_type=F32))
    sel = aff + rb_col
    row = lambda a, e: a[e:e + 1, :]
    n = EXPERTS_PER_GROUP
    best, gi = None, None
    for g in range(N_GROUPS):
        xs = [row(sel, g * n + j) for j in range(n)]
        score = None
        for a in range(n):
            for b in range(a + 1, n):
                pair = xs[a] + xs[b]
                score = pair if score is None else jnp.maximum(score, pair)
        if g == 0:
            best, gi = score, jnp.zeros((1, tm), jnp.int32)
        else:
            upd = score > best
            gi = jnp.where(upd, g, gi)
            best = jnp.where(upd, score, best)

    def pick(a, j):
        out = row(a, j)
        for g in range(1, N_GROUPS):
            out = jnp.where(gi == g, row(a, g * n + j), out)
        return out

    sel_in = [pick(sel, j) for j in range(n)]
    aff_in = [pick(aff, j) for j in range(n)]

    def argmax_first(vals):
        bv, bi = vals[0], jnp.zeros((1, tm), jnp.int32)
        for j in range(1, n):
            upd = vals[j] > bv
            bi = jnp.where(upd, j, bi)
            bv = jnp.where(upd, vals[j], bv)
        return bi

    i1 = argmax_first(sel_in)
    i2 = argmax_first([jnp.where(i1 == j, -jnp.inf, sel_in[j]) for j in range(n)])

    def take(vals, idx):
        out = vals[0]
        for j in range(1, n):
            out = jnp.where(idx == j, vals[j], out)
        return out

    w1, w2 = take(aff_in, i1), take(aff_in, i2)
    wsum = w1 + w2
    w1, w2 = w1 / wsum, w2 / wsum
    rows = [jnp.ones((1, tm), F32)]
    for e in range(N_EXPERTS):
        g, j = divmod(e, n)
        in_g = jnp.where(i1 == j, w1, 0.0) + jnp.where(i2 == j, w2, 0.0)
        rows.append(jnp.where(gi == g, in_g, 0.0))
    return rows, gi


MOE_TILE = 1024
MOE_CHUNK = 288
MOE_MAX_CHUNKS = -(-MOE_TILE // MOE_CHUNK)
MOE_KEY_COL = N_EXPERTS + 1


def _moe_kernel(v_ref, h_ref, rw_ref, rb_ref, wg_ref, wu_ref, wd_ref, g2_ref, o_ref, acc_ref, gate_ref, xb_ref,
                key_ref, xg_ref, yg_ref, gg_ref, nch_ref, *, nsub):
    e = pl.program_id(1)
    tm, ck = MOE_TILE, MOE_CHUNK

    @pl.when(e == 0)
    def _():
        x = v_ref[...]
        xb = x.astype(BF16)
        xb_ref[...] = xb
        rows, gi = _router_gates_t(x, rw_ref[...], rb_ref[...])
        grp = lax.broadcasted_iota(jnp.int32, (8, tm), 0)
        member = grp == gi
        earlier = (lax.broadcasted_iota(jnp.int32, (tm, tm), 0)
                   < lax.broadcasted_iota(jnp.int32, (tm, tm), 1)).astype(BF16)
        before = _dot(member.astype(BF16), earlier)
        rank = jnp.sum(jnp.where(member, before, 0.0), axis=0, keepdims=True)
        key = gi * tm + rank.astype(jnp.int32)
        key_ref[...] = jnp.broadcast_to(key, key_ref.shape)
        rows = rows + [key.astype(F32), jnp.zeros((LANES - 2 - N_EXPERTS, tm), F32)]
        gate_ref[...] = jnp.concatenate(rows, axis=0).T
        cnt = jnp.sum(member.astype(F32), axis=1, keepdims=True)
        nch = sum(jnp.where(cnt > m * ck, 1, 0) for m in range(MOE_MAX_CHUNKS))
        for g in range(N_GROUPS):
            nch_ref[g] = nch[g, 0]
        act = _silu(_dot(xb, wg_ref[0])) * _dot(xb, wu_ref[0])
        acc_ref[...] = _dot(act.astype(BF16), wd_ref[0])

    g = jnp.maximum(e - 1, 0) // EXPERTS_PER_GROUP
    j = jnp.maximum(e - 1, 0) % EXPERTS_PER_GROUP
    n_chunks = jnp.where(e > 0, nch_ref[g], 0)
    key0 = g * tm
    chunk_rows = lambda c: pl.ds(pl.multiple_of(c * ck, 16), ck)

    def gather(c, carry):
        want = lax.broadcasted_iota(jnp.int32, (ck, tm), 0) + (key0 + c * ck)
        pick = jnp.where(key_ref[0:1, :] == want, 1.0, 0.0).astype(BF16)
        xg_ref[chunk_rows(c), :] = _dot(pick, xb_ref[...]).astype(BF16)
        gg_ref[chunk_rows(c), :] = _dot_01(pick, gate_ref[...])
        yg_ref[chunk_rows(c), :] = jnp.zeros((ck, D_MODEL), F32)
        return carry

    lax.fori_loop(0, jnp.where(j == 0, n_chunks, 0), gather, 0)

    def expert(c, carry):
        xc = xg_ref[chunk_rows(c), :]
        lane = lax.broadcasted_iota(jnp.int32, (ck, LANES), 1)
        gate = jnp.sum(jnp.where(lane == e, gg_ref[chunk_rows(c), :], 0.0), axis=1, keepdims=True)
        act = _silu(_dot(xc, wg_ref[0])) * _dot(xc, wu_ref[0]) * gate
        yg_ref[chunk_rows(c), :] += _dot(act.astype(BF16), wd_ref[0])
        return carry

    lax.fori_loop(0, n_chunks, expert, 0)

    def scatter(c, carry):
        want = lax.broadcasted_iota(jnp.int32, (tm, ck), 1) + (key0 + c * ck)
        key_col = gate_ref[:, MOE_KEY_COL:MOE_KEY_COL + 1].astype(jnp.int32)
        place = jnp.where(key_col == want, 1.0, 0.0).astype(BF16)
        acc_ref[...] += _dot(place, yg_ref[chunk_rows(c), :].astype(BF16))
        return carry

    lax.fori_loop(0, jnp.where(j == EXPERTS_PER_GROUP - 1, n_chunks, 0), scatter, 0)

    @pl.when(e == N_EXPERTS)
    def _():
        for s in range(nsub):
            rows = slice(s * TOK_BLK, (s + 1) * TOK_BLK)
            o_ref[rows, :] = h_ref[rows, :] + g2_ref[s] * _rms(acc_ref[rows, :])


def _moe(v, h, rw_t, rb_col, wg, wu, wd, g2):
    rows = v.shape[0]
    tm = MOE_TILE
    nsub = tm // TOK_BLK
    assert rows % tm == 0
    ne = wg.shape[0]
    cap = MOE_MAX_CHUNKS * MOE_CHUNK
    return pl.pallas_call(
        functools.partial(_moe_kernel, nsub=nsub),
        grid=(rows // tm, ne),
        in_specs=[pl.BlockSpec((tm, D_MODEL), lambda i, e: (i, 0)),
                  pl.BlockSpec((tm, D_MODEL), lambda i, e: (i, 0)),
                  pl.BlockSpec(rw_t.shape, lambda i, e: (0, 0)),
                  pl.BlockSpec(rb_col.shape, lambda i, e: (0, 0)),
                  pl.BlockSpec((1, D_MODEL, EXPERT_FF), lambda i, e: (e, 0, 0)),
                  pl.BlockSpec((1, D_MODEL, EXPERT_FF), lambda i, e: (e, 0, 0)),
                  pl.BlockSpec((1, EXPERT_FF, D_MODEL), lambda i, e: (e, 0, 0)),
                  pl.BlockSpec((nsub, 1, D_MODEL), lambda i, e: (i, 0, 0))],
        out_specs=pl.BlockSpec((tm, D_MODEL), lambda i, e: (i, 0)),
        out_shape=jax.ShapeDtypeStruct((rows, D_MODEL), F32),
        scratch_shapes=[pltpu.VMEM((tm, D_MODEL), F32), pltpu.VMEM((tm, LANES), F32),
                        pltpu.VMEM((tm, D_MODEL), BF16), pltpu.VMEM((8, tm), jnp.int32),
                        pltpu.VMEM((cap, D_MODEL), BF16), pltpu.VMEM((cap, D_MODEL), F32),
                        pltpu.VMEM((cap, LANES), F32), pltpu.SMEM((N_GROUPS,), jnp.int32)],
        compiler_params=_cparams("arbitrary", "arbitrary"),
        name="moe",
    )(v, h, rw_t, rb_col, wg, wu, wd, g2)


MLA_KPAD = LANES


def _rope_padded(x, cos, sin):
    n = x.shape[1] // MLA_KPAD
    lane = lax.broadcasted_iota(jnp.int32, (1, x.shape[1]), 1) % MLA_KPAD
    is_x1 = (lane >= MLA_NOPE) & (lane < MLA_NOPE + MLA_ROPE // 2)
    half = MLA_ROPE // 2
    partner = jnp.where(is_x1, pltpu.roll(x, shift=x.shape[1] - half, axis=1), pltpu.roll(x, shift=half, axis=1))
    return x * jnp.tile(cos, (1, n)) + partner * jnp.tile(sin, (1, n))


def _mla_prep_kernel(m_ref, qn_ref, kvn_ref, wq_ref, wk_ref, wv_ref, cos_ref, sin_ref, qt_ref, k_ref, vt_ref):
    cos, sin = cos_ref[...], sin_ref[...]
    ql = (_rms(m_ref[:, :MLA_Q_RANK]) * qn_ref[...]).astype(BF16)
    q = _rope_padded(_dot(ql, wq_ref[...]), cos, sin) * MLA_SCALE
    qt_ref[...] = q.T.astype(qt_ref.dtype)
    kvl = (_rms(m_ref[:, MLA_Q_RANK:MLA_Q_RANK + MLA_KV_RANK]) * kvn_ref[...]).astype(BF16)
    kr = _rope_padded(m_ref[:, MLA_Q_RANK + MLA_KV_RANK:], cos, sin)
    k_ref[...] = (_dot(kvl, wk_ref[...]) + jnp.tile(kr, (1, MLA_HEADS))).astype(k_ref.dtype)
    vt_ref[...] = _dot(kvl, wv_ref[...]).T.astype(vt_ref.dtype)


def _rope_tables(t_len):
    n_lat = t_len - CTX_LEN
    pos = jnp.arange(n_lat)
    n_freq = MLA_ROPE // 4
    inv = ROPE_BASE ** (-jnp.arange(n_freq, dtype=F32) / n_freq)
    ang = jnp.concatenate([(pos // GRID_W).astype(F32)[:, None] * inv, (pos % GRID_W).astype(F32)[:, None] * inv],
                          axis=-1)
    cos = jnp.concatenate([jnp.ones((CTX_LEN, MLA_ROPE // 2), F32), jnp.cos(ang)], axis=0)
    sin = jnp.concatenate([jnp.zeros((CTX_LEN, MLA_ROPE // 2), F32), jnp.sin(ang)], axis=0)
    tail = MLA_KPAD - MLA_NOPE - MLA_ROPE
    cos_t = jnp.concatenate([jnp.ones((t_len, MLA_NOPE), F32), cos, cos, jnp.ones((t_len, tail), F32)], axis=1)
    sin_t = jnp.concatenate([jnp.zeros((t_len, MLA_NOPE), F32), -sin, sin, jnp.zeros((t_len, tail), F32)], axis=1)
    return cos_t, sin_t


def _mla_prep(m, qn, kvn, wq, wk, wv, bsz):
    rows, width = m.shape
    t = rows // bsz
    tm = TOK_BLK
    nb = t // tm
    cos_t, sin_t = _rope_tables(t)
    pos = pl.BlockSpec((tm, MLA_KPAD), lambda i: (i % nb, 0))
    col = lambda n: pl.BlockSpec((n, tm), lambda i: (0, i))
    nk, nv = MLA_HEADS * MLA_KPAD, MLA_HEADS * MLA_V
    return pl.pallas_call(
        _mla_prep_kernel,
        grid=(rows // tm,),
        in_specs=[_row_spec(tm, width), _full_spec(qn.shape), _full_spec(kvn.shape), _full_spec(wq.shape),
                  _full_spec(wk.shape), _full_spec(wv.shape), pos, pos],
        out_specs=[col(nk), _row_spec(tm, nk), col(nv)],
        out_shape=[jax.ShapeDtypeStruct((nk, rows), BF16), jax.ShapeDtypeStruct((rows, nk), BF16),
                   jax.ShapeDtypeStruct((nv, rows), BF16)],
        compiler_params=_cparams("arbitrary"),
        name="mla_prep",
    )(m, qn, kvn, wq, wk, wv, cos_t, sin_t)


def _attn_kernel(qt_ref, k_ref, vt_ref, o_ref):
    def scores(hd):
        return _dot(k_ref[0, :, hd * MLA_KPAD:(hd + 1) * MLA_KPAD], qt_ref[hd * MLA_KPAD:(hd + 1) * MLA_KPAD, :])

    outs = []
    s_next = scores(0)
    for hd in range(MLA_HEADS):
        s = s_next
        if hd + 1 < MLA_HEADS:
            s_next = scores(hd + 1)
        p = jnp.exp(s - jnp.max(s, axis=0, keepdims=True))
        o = _dot(vt_ref[hd * MLA_V:(hd + 1) * MLA_V, :], p.astype(BF16))
        outs.append(o / jnp.sum(p, axis=0, keepdims=True))
    o_ref[...] = jnp.concatenate(outs, axis=0).T.astype(o_ref.dtype)


def _lat_block(i, nb):
    return (i // (nb - 1)) * nb + 1 + i % (nb - 1)


def _attention(qt, k, vt, bsz):
    t = k.shape[1]
    tq = TOK_BLK
    nb = t // tq
    n_lat = bsz * (nb - 1)
    return pl.pallas_call(
        _attn_kernel,
        grid=(n_lat,),
        in_specs=[pl.BlockSpec((qt.shape[0], tq), lambda i: (0, _lat_block(i, nb))),
                  pl.BlockSpec((1, t, k.shape[2]), lambda i: (i // (nb - 1), 0, 0)),
                  pl.BlockSpec((vt.shape[0], t), lambda i: (0, i // (nb - 1)))],
        out_specs=_row_spec(tq, MLA_HEADS * MLA_V),
        out_shape=jax.ShapeDtypeStruct((n_lat * tq, MLA_HEADS * MLA_V), BF16),
        compiler_params=_cparams("arbitrary"),
        name="mla_attention",
    )(qt, k, vt)


def _seg_sum(x):
    i = lax.broadcasted_iota(jnp.int32, (2 * RW_N, 2 * RW_N), 0) // RW_N
    j = lax.broadcasted_iota(jnp.int32, (2 * RW_N, 2 * RW_N), 1) // RW_N
    return _dot_x01(x, (i == j).astype(BF16))


RW_GC = 6


def _rwkv_precompute(it, r_s, v_s, a_s, lw_s, kd_s, bb_s, gam_s, y0_s, phi_s, psi_s, dec_s):
    masks = [_tri_masks(0), _tri_masks(1)]
    hp = 2 * RW_N
    ch = []
    for cc in range(RW_GC):
        c = it * RW_GC + cc
        rows = pl.ds(pl.multiple_of(c * CHUNK, CHUNK), CHUNK)
        rows2 = pl.ds(pl.multiple_of(c * hp, hp), hp)
        r, v, a = r_s[rows, :], v_s[rows, :], a_s[rows, :]
        for d in range(2):
            last = CHUNK - 1 if d == 0 else 0
            lw, kd, bb = lw_s[d, rows, :], kd_s[d, rows, :], bb_s[d, rows, :]
            cl = _dot_01(masks[d][0].astype(BF16), lw)
            cl_tot = cl[last:last + 1, :]
            p_inv, p_end = jnp.exp(-cl), jnp.exp(cl_tot - cl)
            at, rt = a * jnp.exp(cl - lw), r * jnp.exp(cl)
            bt, kt, bh, kh = bb * p_inv, kd * p_inv, bb * p_end, kd * p_end
            dec_s[d, rows2, :] = jnp.broadcast_to(jnp.exp(cl_tot), (hp, hp)).T
            for hh in range(2):
                cols = slice(hh * RW_N, (hh + 1) * RW_N)
                ch.append(dict(
                    d=d, rows=rows, rows2=rows2, at=at[:, cols], rt=rt[:, cols], v=v[:, cols],
                    ar=jnp.concatenate([at[:, cols], rt[:, cols]], axis=0).astype(BF16),
                    bk=jnp.concatenate([bt[:, cols], kt[:, cols]], axis=0).astype(BF16),
                    bkt=jnp.concatenate([bh[:, cols], kh[:, cols]], axis=1).T.astype(BF16)))
    for x in ch:
        x["m4"] = _dot_t(x["ar"], x["bk"])
    for x in ch:
        incl, strict = masks[x["d"]]
        m4 = x.pop("m4")
        x["vb"] = x["v"].astype(BF16)
        x["pw"] = jnp.where(strict, m4[:CHUNK, :CHUNK], 0.0)
        x["aak"] = jnp.where(strict, m4[:CHUNK, CHUNK:], 0.0).astype(BF16)
        x["rb"] = jnp.where(incl, m4[CHUNK:, :CHUNK], 0.0).astype(BF16)
        x["rk"] = jnp.where(incl, m4[CHUNK:, CHUNK:], 0.0).astype(BF16)
    for x in ch:
        x["x"] = jnp.concatenate([_dot(x.pop("aak"), x["vb"]), x.pop("at")], axis=1)
    step = CHUNK // 2
    while True:
        for x in ch:
            x["x"] = x["x"] + _dot(x["pw"].astype(BF16), x["x"].astype(BF16))
        step //= 2
        if step == 0:
            break
        for x in ch:
            pb = x["pw"].astype(BF16)
            x["pw"] = _dot(pb, pb)
    for x in ch:
        xb = x["x"].astype(BF16)
        bkt = x.pop("bkt")
        x["rbx"] = _dot(x.pop("rb"), xb)
        x["rkv"] = _dot(x.pop("rk"), x["vb"])
        x["bx"] = _dot(bkt[:RW_N], xb)
        x["kv"] = _dot(bkt[RW_N:], x["vb"])
    zero = jnp.zeros((RW_N, RW_N), F32)
    lanes = lambda a, b: jnp.concatenate([a, b], axis=1)
    bdiag = lambda a, b: jnp.concatenate([lanes(a, zero), lanes(zero, b)], axis=0)
    for j in range(0, len(ch), 2):
        p0, p1 = ch[j], ch[j + 1]
        d, rows, rows2 = p0["d"], p0["rows"], p0["rows2"]
        gam_s[d, rows, :] = lanes(p0["rt"] + p0["rbx"][:, RW_N:], p1["rt"] + p1["rbx"][:, RW_N:]).astype(BF16)
        y0 = lanes(p0["rbx"][:, :RW_N] + p0["rkv"], p1["rbx"][:, :RW_N] + p1["rkv"])
        if d == 0:
            y0_fwd = y0
        else:
            y0_s[rows, :] = y0_fwd + y0
        phi_s[d, rows2, :] = bdiag(p0["bx"][:, RW_N:], p1["bx"][:, RW_N:]).astype(BF16)
        psi_s[d, rows2, :] = bdiag(p0["bx"][:, :RW_N] + p0["kv"], p1["bx"][:, :RW_N] + p1["kv"])


def _rwkv_step(i, states, phi_s, psi_s, dec_s, st_s, n_chunks, n_ctx):
    hp = 2 * RW_N
    loads = []
    for d in range(2):
        c = i if d == 0 else _chunk_bwd(i, n_ctx, n_chunks)
        rows2 = pl.ds(pl.multiple_of(c * hp, hp), hp)
        loads.append((rows2, phi_s[d, rows2, :], psi_s[d, rows2, :], dec_s[d, rows2, :]))
    new_states = []
    for d in range(2):
        rows2, phi, psi, dec = loads[d]
        hb = states[d].astype(BF16)
        st_s[d, rows2, :] = hb
        new_states.append(states[d] * dec + _dot(phi, hb) + psi)
    return tuple(new_states)


def _rwkv_output(it, gam_s, y0_s, st_s, bon_s, gate_s, ln_ref, o_ref):
    hp = 2 * RW_N
    items = []
    for cc in range(RW_GC):
        c = it * RW_GC + cc
        rows = pl.ds(pl.multiple_of(c * CHUNK, CHUNK), CHUNK)
        rows2 = pl.ds(pl.multiple_of(c * hp, hp), hp)
        items.append(dict(rows=rows, gam=[gam_s[d, rows, :] for d in range(2)],
                          st=[st_s[d, rows2, :] for d in range(2)], y0=y0_s[rows, :],
                          bon=bon_s[rows, :], gate=gate_s[rows, :]))
    for x in items:
        x["o"] = _dot(x["gam"][0], x["st"][0]) + _dot(x["gam"][1], x["st"][1]) + x["y0"]
    for x in items:
        x["ms"] = _seg_sum(x["o"] * x["o"]) * (1.0 / RW_N)
    for x in items:
        y = x["o"] * lax.rsqrt(x["ms"] + EPS) * ln_ref[...] + x["bon"]
        o_ref[0, x["rows"], :] = y * x["gate"]


def _rwkv_kernel(r_ref, k_ref, v_ref, low_ref, mur_ref, muk_ref, muv_ref, mul_ref, w0_ref, w2_ref, a0_ref,
                 a2_ref, g2_ref, kk_ref, ka_ref, rk_ref, ln_ref, o_ref,
                 r_s, v_s, a_s, bon_s, gate_s, lw_s, kd_s, bb_s, gam_s, y0_s, phi_s, psi_s, dec_s, st_s,
                 *, n_chunks, n_ctx):
    t_len = r_s.shape[0]
    t = lax.broadcasted_iota(jnp.int32, (t_len, 1), 0)
    lo = jnp.where(t < CTX_LEN, 0, CTX_LEN)
    hi = jnp.where(t < CTX_LEN, CTX_LEN, t_len)

    def shift_mix(x, mu):
        prev = jnp.where(t - 1 >= lo, pltpu.roll(x, shift=1, axis=0), 0.0)
        nxt = jnp.where(t + 1 < hi, pltpu.roll(x, shift=t_len - 1, axis=0), 0.0)
        return x + mu * (0.5 * (prev + nxt) - x)

    r = shift_mix(r_ref[0], mur_ref[...])
    k = shift_mix(k_ref[0], muk_ref[...])
    v = shift_mix(v_ref[0], muv_ref[...])
    low = shift_mix(low_ref[0], mul_ref[...])
    kk = k * kk_ref[...]
    kk = kk / jnp.maximum(jnp.sqrt(_seg_sum(kk * kk)), 1e-12)
    r_s[...] = r
    v_s[...] = v
    a_s[...] = -kk
    bonus = jnp.zeros_like(r)
    for d in range(2):
        wl = jnp.tanh(low[:, d * RW_DECAY_RANK:(d + 1) * RW_DECAY_RANK])
        al = low[:, 2 * RW_DECAY_RANK + d * RW_A_RANK:2 * RW_DECAY_RANK + (d + 1) * RW_A_RANK]
        wz = w0_ref[d:d + 1, :] + _dot(wl.astype(BF16), w2_ref[d])
        lw_s[d] = -jnp.exp(-_softplus(-wz) - 0.5)
        a = jax.nn.sigmoid(a0_ref[d:d + 1, :] + _dot(al.astype(BF16), a2_ref[d]))
        kd = k * (1.0 + (a - 1.0) * ka_ref[...])
        kd_s[d] = kd
        bb_s[d] = kk * a
        bonus = bonus + _seg_sum(r * kd * rk_ref[...])
    bon_s[...] = bonus * v
    gl = jax.nn.sigmoid(low[:, 2 * RW_DECAY_RANK + 2 * RW_A_RANK:])
    gate_s[...] = _dot(gl.astype(BF16), g2_ref[...])

    def precompute(it, carry):
        _rwkv_precompute(it, r_s, v_s, a_s, lw_s, kd_s, bb_s, gam_s, y0_s, phi_s, psi_s, dec_s)
        return carry

    lax.fori_loop(0, n_chunks // RW_GC, precompute, 0)
    zero = jnp.zeros((2 * RW_N, 2 * RW_N), F32)
    lax.fori_loop(0, n_chunks, lambda i, st: _rwkv_step(i, st, phi_s, psi_s, dec_s, st_s, n_chunks, n_ctx),
                  (zero, zero))

    def output(it, carry):
        _rwkv_output(it, gam_s, y0_s, st_s, bon_s, gate_s, ln_ref, o_ref)
        return carry

    lax.fori_loop(0, n_chunks // RW_GC, output, 0)


def _rwkv(rkv, low, mu, w0, w2, a0, a2, g2, kk, ka, rk, ln):
    bsz, t, _ = rkv.shape
    hp = 2 * RW_N
    n_hp = RW_W // hp
    n_low = low.shape[2]
    tok = lambda off: pl.BlockSpec((1, t, hp), lambda b, j: (b, 0, off * n_hp + j))
    vec = lambda off: pl.BlockSpec((1, hp), lambda b, j: (0, off * n_hp + j))
    par2 = pl.BlockSpec((2, hp), lambda b, j: (0, j))
    par3 = pl.BlockSpec((2, RW_DECAY_RANK, hp), lambda b, j: (0, 0, j))
    tbuf = lambda: pltpu.VMEM((t, hp), F32)
    dbuf = lambda: pltpu.VMEM((2, t, hp), F32)
    mu_rkv, mu_low = mu[None, :3 * RW_W], mu[None, 3 * RW_W:]
    row = lambda a: a.reshape(1, RW_W)
    return pl.pallas_call(
        functools.partial(_rwkv_kernel, n_chunks=t // CHUNK, n_ctx=CTX_LEN // CHUNK),
        grid=(bsz, n_hp),
        in_specs=[tok(0), tok(1), tok(2), pl.BlockSpec((1, t, n_low), lambda b, j: (b, 0, 0)),
                  vec(0), vec(1), vec(2), pl.BlockSpec((1, n_low), lambda b, j: (0, 0)),
                  par2, par3, par2, par3, pl.BlockSpec((RW_GATE_RANK, hp), lambda b, j: (0, j)),
                  vec(0), vec(0), vec(0), vec(0)],
        out_specs=pl.BlockSpec((1, t, hp), lambda b, j: (b, 0, j)),
        out_shape=jax.ShapeDtypeStruct((bsz, t, RW_W), F32),
        scratch_shapes=[tbuf(), tbuf(), tbuf(), tbuf(), tbuf(), dbuf(), dbuf(), dbuf(),
                        pltpu.VMEM((2, t, hp), BF16), tbuf(), pltpu.VMEM((2, 2 * t, hp), BF16), pltpu.VMEM((2, 2 * t, hp), F32),
                        pltpu.VMEM((2, 2 * t, hp), F32), pltpu.VMEM((2, 2 * t, hp), BF16)],
        compiler_params=_cparams("arbitrary", "arbitrary"),
        name="rwkv7",
    )(rkv, rkv, rkv, low, mu_rkv, mu_rkv, mu_rkv, mu_low, w0, w2.astype(BF16), a0, a2.astype(BF16),
      g2.astype(BF16), row(kk), row(ka), row(rk), row(ln))


def _cd_out_kernel(att_ref, rw_ref, h_ref, w_ref, g1_ref, a2_ref, b2_ref, h_out, v_out):
    n_att = MLA_HEADS * MLA_V
    y = _dot(att_ref[...], w_ref[:n_att, :]) + _dot(rw_ref[...].astype(BF16), w_ref[n_att:, :])
    _residual_and_next(y, h_ref, g1_ref, a2_ref, b2_ref, h_out, v_out)


def _cd_out(att, rw, h, w_out, g1, a2, b2, nb):
    rows = att.shape[0]
    tm = TOK_BLK
    lat = lambda: pl.BlockSpec((tm, D_MODEL), lambda i: (_lat_block(i, nb), 0))
    return pl.pallas_call(
        _cd_out_kernel,
        grid=(rows // tm,),
        in_specs=[_row_spec(tm, att.shape[1]), lat(), lat(), _full_spec(w_out.shape), _tab_spec(), _tab_spec(),
                  _tab_spec()],
        out_specs=[_row_spec(tm, D_MODEL), _row_spec(tm, D_MODEL)],
        out_shape=[jax.ShapeDtypeStruct((rows, D_MODEL), F32)] * 2,
        compiler_params=_cparams("arbitrary"),
        name="cd_out",
    )(att, rw, h, w_out, g1, a2, b2)


def _pad_cols(w, width):
    return jnp.pad(w, ((0, 0), (0, width - w.shape[1])))


def _split_cols(w, sizes):
    assert sum(sizes) == w.shape[1]
    out, start = [], 0
    for s in sizes:
        out.append(w[:, start:start + s])
        start += s
    return out


def _layer_ab(h, tabs, ab_w_in, ab_w_out, ml_i_bias, ml_f_bias, ml_norm, ssd_conv_w, ssd_conv_b, ssd_dt_bias,
              ssd_a_log, ssd_d, ssd_norm, bsz):
    a1, b1, g1, a2, b2, _ = tabs
    t = h.shape[0] // bsz
    q, k, v, og, ig, fg, z, xbc, dt = _split_cols(
        ab_w_in, (ML_QKW, ML_QKW, ML_W, ML_W, 2 * ML_HEADS, 2 * ML_HEADS, SSD_W, SSD_XBC, 2 * SSD_HEADS))
    small = _pad_cols(jnp.concatenate([ig, fg, dt], axis=1), LANES)
    w_in = jnp.concatenate([q, k, v, og, z, xbc, small], axis=1).astype(BF16)
    widths = (2 * ML_QKW + ML_W, ML_W, SSD_W, SSD_XBC, LANES)
    starts = [0]
    for wd in widths[:-1]:
        starts.append(starts[-1] + wd)
    qkv, og_p, z_p, xbc_p, small_p = _norm_proj(h, a1, b1, w_in, tuple(zip(starts, widths)),
                                               (BF16, F32, F32, F32, F32), "ab_in_proj")
    pad = LANES - 4 * ML_HEADS
    bias_row = jnp.concatenate([ml_i_bias.reshape(-1), ml_f_bias.reshape(-1), jnp.zeros((pad,), F32)])[None]
    hf, hb = _mlstm(qkv.reshape(bsz, t, -1), small_p.reshape(bsz, t, LANES), bias_row)
    xbc_c = _ssd_conv(xbc_p.reshape(bsz, t, SSD_XBC), ssd_conv_w, ssd_conv_b)
    lead = jnp.zeros((4 * ML_HEADS,), F32)
    tail = jnp.zeros((LANES - 4 * ML_HEADS - 2 * SSD_HEADS,), F32)
    dtb_row = jnp.concatenate([lead, ssd_dt_bias.reshape(-1), tail])[None]
    ea_row = jnp.concatenate([lead, jnp.exp(ssd_a_log).reshape(-1), tail])[None]
    yf, yb = _ssd(xbc_c, small_p.reshape(bsz, t, LANES), dtb_row, ea_row)
    flat = lambda a: a.reshape(bsz * t, -1)
    return _ab_out(flat(hf), flat(hb), og_p, flat(yf), flat(yb), flat(xbc_c), z_p, h, ml_norm[None],
                   jnp.repeat(ssd_d, SSD_P)[None], ssd_norm[None], ab_w_out.astype(BF16), g1, a2, b2)


def _pad_heads(w, n_heads, take, width):
    k, n = w.shape
    w3 = w.reshape(k, n_heads, n // n_heads)[:, :, take]
    return jnp.pad(w3, ((0, 0), (0, 0), (0, width - w3.shape[2]))).reshape(k, n_heads * width)


def _layer_cd(h, tabs_all, tabs_lat, cd_w_in, cd_w_out, mla_qn, mla_w_uq, mla_kvn, mla_w_ukv, rw_mu, rw_w0, rw_w2,
              rw_a0, rw_a2, rw_g2, rw_kk, rw_ka, rw_rk, rw_ln, bsz):
    a1, b1 = tabs_all[0], tabs_all[1]
    _, _, g1, a2, b2, _ = tabs_lat
    t = h.shape[0] // bsz
    n_lat = MLA_Q_RANK + MLA_KV_RANK
    n_mla = n_lat + MLA_ROPE
    n_low = 2 * RW_DECAY_RANK + 2 * RW_A_RANK + RW_GATE_RANK
    mla_w = n_lat + MLA_KPAD
    k_rope = jnp.pad(cd_w_in[:, n_lat:n_mla], ((0, 0), (MLA_NOPE, MLA_KPAD - MLA_NOPE - MLA_ROPE)))
    w_in = jnp.concatenate([cd_w_in[:, :n_lat], k_rope, cd_w_in[:, n_mla:]], axis=1).astype(BF16)
    segs = ((0, mla_w), (mla_w, 3 * RW_W), (mla_w + 3 * RW_W, n_low))
    m, rkv, low = _norm_proj(h, a1, b1, w_in, segs, (F32, F32, F32), "cd_in_proj")
    wq = _pad_heads(mla_w_uq, MLA_HEADS, slice(None), MLA_KPAD).astype(BF16)
    wk = _pad_heads(mla_w_ukv, MLA_HEADS, slice(0, MLA_NOPE), MLA_KPAD).astype(BF16)
    wv = _pad_heads(mla_w_ukv, MLA_HEADS, slice(MLA_NOPE, None), MLA_V).astype(BF16)
    qt, k, vt = _mla_prep(m, mla_qn[None], mla_kvn[None], wq, wk, wv, bsz)
    per_b = lambda a: a.reshape(bsz, t, -1)
    att = _attention(qt, per_b(k), vt, bsz)
    rw = _rwkv(per_b(rkv), per_b(low), rw_mu, rw_w0, rw_w2, rw_a0, rw_a2, rw_g2, rw_kk, rw_ka, rw_rk, rw_ln)
    return _cd_out(att, rw.reshape(bsz * t, RW_W), h, cd_w_out.astype(BF16), g1, a2, b2, t // TOK_BLK)


def _moe_weights(router_w, router_bias, exp_w_gate, exp_w_up, exp_w_down, sh_w_gate, sh_w_up, sh_w_down):
    stack = lambda s, e: jnp.concatenate([s[None], e], axis=0).astype(BF16)
    return (router_w.T, router_bias[:, None], stack(sh_w_gate, exp_w_gate), stack(sh_w_up, exp_w_up),
            stack(sh_w_down, exp_w_down))


def kernel(x, c, ctx, c_ctx, w_mod, b_mod, norm_g, ab_w_in, ab_w_out, ml_i_bias, ml_f_bias, ml_norm, ssd_conv_w, ssd_conv_b, ssd_dt_bias, ssd_a_log, ssd_d, ssd_norm, cd_w_in, cd_w_out, mla_qn, mla_w_uq, mla_kvn, mla_w_ukv, rw_mu, rw_w0, rw_w2, rw_a0, rw_a2, rw_g2, rw_kk, rw_ka, rw_rk, rw_ln, router_w, router_bias, exp_w_gate, exp_w_up, exp_w_down, sh_w_gate, sh_w_up, sh_w_down):
    bsz, seq, _ = x.shape
    t = CTX_LEN + seq
    h = jnp.concatenate([ctx, x], axis=1).reshape(bsz * t, D_MODEL)

    mod = _modulation(c, c_ctx, w_mod[0], b_mod[0])
    tabs = _block_tables(mod, norm_g[0], bsz, t // TOK_BLK, True)
    h, v = _layer_ab(h, tabs, ab_w_in[0], ab_w_out[0], ml_i_bias[0], ml_f_bias[0], ml_norm[0], ssd_conv_w[0],
                     ssd_conv_b[0], ssd_dt_bias[0], ssd_a_log[0], ssd_d[0], ssd_norm[0], bsz)
    moe_w = _moe_weights(router_w, router_bias, exp_w_gate[0], exp_w_up[0], exp_w_down[0], sh_w_gate[0],
                         sh_w_up[0], sh_w_down[0])
    h = _moe(v, h, *moe_w, tabs[5])

    mod = _modulation(c, c_ctx, w_mod[1], b_mod[1])
    nb = t // TOK_BLK
    tabs_all = _block_tables(mod, norm_g[1], bsz, nb, True)
    tabs_lat = _block_tables(mod, norm_g[1], bsz, nb - 1, False)
    h, v = _layer_cd(h, tabs_all, tabs_lat, cd_w_in[0], cd_w_out[0], mla_qn[0], mla_w_uq[0], mla_kvn[0],
                     mla_w_ukv[0], rw_mu[0], rw_w0[0], rw_w2[0], rw_a0[0], rw_a2[0], rw_g2[0], rw_kk[0], rw_ka[0],
                     rw_rk[0], rw_ln[0], bsz)
    moe_w = _moe_weights(router_w, router_bias, exp_w_gate[1], exp_w_up[1], exp_w_down[1], sh_w_gate[1],
                         sh_w_up[1], sh_w_down[1])
    h = _moe(v, h, *moe_w, tabs_lat[5])
    return h.reshape(bsz, seq, D_MODEL)
```

```python
import functools

import jax
import jax.numpy as jnp
from jax import lax
from jax.experimental import pallas as pl
from jax.experimental.pallas import tpu as pltpu

F32 = jnp.float32
BF16 = jnp.bfloat16
HIGHEST = lax.Precision.HIGHEST

D_MODEL = 1024
CTX_LEN = 256
GRID_W = 64
EPS = 1e-6
ROPE_BASE = 10000.0
TOK_BLK = 256
CHUNK = 64
SCAN_ROWS = 4
LANES = 128

ML_HEADS, ML_QK, ML_V = 4, 128, 256
ML_QKW, ML_W = ML_HEADS * ML_QK, ML_HEADS * ML_V
SSD_HEADS, SSD_P, SSD_N, SSD_GROUPS, SSD_CONV = 16, 64, 64, 2, 5
SSD_HPG = SSD_HEADS // SSD_GROUPS
SSD_W, SSD_GN = SSD_HEADS * SSD_P, SSD_GROUPS * SSD_N
SSD_XBC = SSD_W + 2 * SSD_GN
MLA_HEADS, MLA_NOPE, MLA_ROPE, MLA_V = 8, 64, 32, 64
MLA_Q_RANK, MLA_KV_RANK = 384, 256
MLA_SCALE = (MLA_NOPE + MLA_ROPE) ** -0.5
RW_HEADS, RW_N = 16, 64
RW_W = RW_HEADS * RW_N
RW_DECAY_RANK, RW_A_RANK, RW_GATE_RANK = 64, 64, 128
N_EXPERTS, N_GROUPS, EXPERT_FF = 16, 4, 512
EXPERTS_PER_GROUP = N_EXPERTS // N_GROUPS

VMEM_LIMIT = 56 * 1024 * 1024


def _cparams(*sem):
    return pltpu.CompilerParams(dimension_semantics=sem, vmem_limit_bytes=VMEM_LIMIT)


def _silu(x):
    return x * jax.nn.sigmoid(x)


def _softplus(x):
    return jnp.maximum(x, 0.0) + jnp.log1p(jnp.exp(-jnp.abs(x)))


def _log_sigmoid(x):
    return jnp.minimum(x, 0.0) - jnp.log1p(jnp.exp(-jnp.abs(x)))


def _rms(x):
    return x * lax.rsqrt(jnp.mean(x * x, axis=-1, keepdims=True) + EPS)


def _dot_t(a, b):
    return lax.dot_general(a, b, (((1,), (1,)), ((), ())), preferred_element_type=F32)


def _dot(a, b):
    return jnp.dot(a, b, preferred_element_type=F32)


def _split3(x):
    hi = x.astype(BF16)
    r1 = x - hi.astype(F32)
    mid = r1.astype(BF16)
    return hi, mid, (r1 - mid.astype(F32)).astype(BF16)


def _dot_01(m, x, terms=3):
    parts = _split3(x)[:terms]
    return sum(_dot(m, p) for p in parts)


def _dot_x01(x, m, terms=3):
    parts = _split3(x)[:terms]
    return sum(_dot(p, m) for p in parts)


def _tri_masks(d):
    row = lax.broadcasted_iota(jnp.int32, (CHUNK, CHUNK), 0)
    col = lax.broadcasted_iota(jnp.int32, (CHUNK, CHUNK), 1)
    incl = (col <= row) if d == 0 else (col >= row)
    strict = (col < row) if d == 0 else (col > row)
    return incl, strict


def _mod_kernel(s_ref, w_ref, b_ref, o_ref):
    s = _silu(s_ref[...])
    o_ref[...] = jnp.dot(s, w_ref[...], precision=HIGHEST, preferred_element_type=F32) + b_ref[...]


def _modulation(c, c_ctx, w, b):
    bsz = c.shape[0]
    rows = 8 * ((bsz + 1 + 7) // 8)
    s = jnp.zeros((rows, D_MODEL), F32).at[:bsz].set(c).at[bsz].set(c_ctx)
    n = w.shape[1]
    return pl.pallas_call(
        _mod_kernel,
        grid=(n // D_MODEL,),
        in_specs=[pl.BlockSpec((rows, D_MODEL), lambda j: (0, 0)),
                  pl.BlockSpec((D_MODEL, D_MODEL), lambda j: (0, j)),
                  pl.BlockSpec((1, D_MODEL), lambda j: (0, j))],
        out_specs=pl.BlockSpec((rows, D_MODEL), lambda j: (0, j)),
        out_shape=jax.ShapeDtypeStruct((rows, n), F32),
        compiler_params=_cparams("arbitrary"),
        name="modulation",
    )(s, w, b.reshape(1, n))


def _block_tables(mod, norm_g, bsz, blocks_per_row, with_ctx):
    j = jnp.arange(blocks_per_row)
    b = jnp.arange(bsz)
    if with_ctx:
        sel = jnp.where(j[None, :] == 0, bsz, b[:, None]).reshape(-1)
    else:
        sel = jnp.broadcast_to(b[:, None], (bsz, blocks_per_row)).reshape(-1)
    m = mod[sel]
    sh_a, sc_a, g_a, sh_f, sc_f, g_f = jnp.split(m, 6, axis=-1)
    tabs = (norm_g[0] * (1 + sc_a), sh_a, g_a * norm_g[1], norm_g[2] * (1 + sc_f), sh_f, g_f * norm_g[3])
    return tuple(t[:, None, :] for t in tabs)


def _norm_proj_kernel(x_ref, a_ref, b_ref, w_ref, *refs, segs, nsub):
    out_refs, u_ref = refs[:len(segs)], refs[len(segs)]
    for s in range(nsub):
        rows = slice(s * TOK_BLK, (s + 1) * TOK_BLK)
        u_ref[rows, :] = (_rms(x_ref[rows, :]) * a_ref[s] + b_ref[s]).astype(BF16)
    u = u_ref[...]
    for o_ref, (start, width) in zip(out_refs, segs):
        o_ref[...] = _dot(u, w_ref[:, start:start + width]).astype(o_ref.dtype)


def _norm_proj(x, a_tab, b_tab, w, segs, dtypes, name):
    rows = x.shape[0]
    nsub = 2
    tm = nsub * TOK_BLK
    assert rows % tm == 0
    n = w.shape[1]
    return pl.pallas_call(
        functools.partial(_norm_proj_kernel, segs=segs, nsub=nsub),
        grid=(rows // tm,),
        in_specs=[pl.BlockSpec((tm, D_MODEL), lambda i: (i, 0)),
                  pl.BlockSpec((nsub, 1, D_MODEL), lambda i: (i, 0, 0)),
                  pl.BlockSpec((nsub, 1, D_MODEL), lambda i: (i, 0, 0)),
                  pl.BlockSpec((D_MODEL, n), lambda i: (0, 0))],
        out_specs=[pl.BlockSpec((tm, wd), lambda i: (i, 0)) for _, wd in segs],
        out_shape=[jax.ShapeDtypeStruct((rows, wd), dt) for (_, wd), dt in zip(segs, dtypes)],
        scratch_shapes=[pltpu.VMEM((tm, D_MODEL), BF16)],
        compiler_params=_cparams("arbitrary"),
        name=name,
    )(x, a_tab, b_tab, w)


def _chunk_fwd(i):
    return i


def _chunk_bwd(i, n_ctx, n_all):
    return jnp.where(i < n_ctx, n_ctx - 1 - i, n_all + n_ctx - 1 - i)


def _mlstm_chunk(gates, refs, c_old, n_old, m_old, bias_ref, seli_ref, selv_ref, selq_ref, selit_ref, nmask_ref):
    scale = ML_QK ** -0.5
    nch = 2 * ML_HEADS
    lane = lax.broadcasted_iota(jnp.int32, (1, LANES), 1)
    fwd_cols, valid = lane < ML_HEADS, lane < nch
    pre, bcum = [], []
    for d in range(2):
        p_d = gates[d] + bias_ref[...]
        pre.append(p_d)
        bcum.append(_dot_01(_tri_masks(d)[0].astype(BF16), _log_sigmoid(p_d)))
    ig = jnp.where(valid, jnp.where(fwd_cols, pre[0], pre[1]), 0.0)
    b_raw = jnp.where(lane < nch + ML_HEADS, bcum[0], bcum[1])
    b_al = jnp.where(valid, pltpu.roll(b_raw, shift=LANES - nch, axis=1), 0.0)
    c_cols = ig - b_al
    row = lax.broadcasted_iota(jnp.int32, (CHUNK, LANES), 0)
    pmax, smax, sh = c_cols, c_cols, 1
    while sh < CHUNK:
        pmax = jnp.maximum(pmax, jnp.where(row >= sh, pltpu.roll(pmax, shift=sh, axis=0), -jnp.inf))
        smax = jnp.maximum(smax, jnp.where(row < CHUNK - sh, pltpu.roll(smax, shift=CHUNK - sh, axis=0), -jnp.inf))
        sh *= 2
    cmax = jnp.where(fwd_cols, pmax, smax)
    ends = lambda a: jnp.where(fwd_cols, a[CHUNK - 1:CHUNK, :], a[0:1, :])
    m_j = b_al + jnp.maximum(m_old, cmax)
    w_int = jnp.exp(b_al + m_old - m_j)
    btot = ends(b_al)
    m_new = btot + jnp.maximum(m_old, ends(cmax))
    e_cols = jnp.exp(btot + c_cols - m_new)
    dec = jnp.exp(btot + m_old - m_new)

    seli, selv = seli_ref[...], selv_ref[...]
    lane5 = lax.broadcasted_iota(jnp.int32, (CHUNK, nch * CHUNK), 1)
    row5 = lax.broadcasted_iota(jnp.int32, (CHUNK, nch * CHUNK), 0)
    ahead = lane5 % CHUNK - row5
    causal = jnp.where(lane5 < ML_HEADS * CHUNK, ahead, -ahead) <= 0
    spread = _dot_x01(jnp.concatenate([b_al, ig, m_j], axis=0), seli, terms=2)
    b_x, ig_x, mj_x = spread[:CHUNK], spread[CHUNK:2 * CHUNK], spread[2 * CHUNK:]
    c_row = _dot_01(jnp.ones((CHUNK, CHUNK), BF16), jnp.where(ahead == 0, ig_x - b_x, 0.0),
                    terms=2)
    p = jnp.exp(jnp.where(causal, b_x + c_row - mj_x, -jnp.inf))

    q = [refs[ci // ML_HEADS][:, (ci % ML_HEADS) * ML_QK:(ci % ML_HEADS + 1) * ML_QK] for ci in range(nch)]
    k = [refs[ci // ML_HEADS][:, ML_QKW + (ci % ML_HEADS) * ML_QK:ML_QKW + (ci % ML_HEADS + 1) * ML_QK]
         for ci in range(nch)]
    v = [refs[ci // ML_HEADS][:, 2 * ML_QKW + (ci % ML_HEADS) * ML_V:2 * ML_QKW + (ci % ML_HEADS + 1) * ML_V]
         for ci in range(nch)]
    s_all = jnp.concatenate([_dot_t(q[ci], k[ci]) for ci in range(nch)], axis=1) * scale * p
    nq = sum(_dot(refs[d][:, :ML_QKW], n_old[d].astype(BF16)) for d in range(2))
    den = _dot_x01(s_all, selit_ref[...]) + w_int * nq * scale
    inv = 1.0 / jnp.maximum(jnp.abs(den), jnp.exp(-m_j))
    s_b = (s_all * _dot_x01(inv, seli, terms=2)).astype(BF16)
    spread = _dot_x01(jnp.concatenate([w_int * inv * scale, e_cols], axis=0), selq_ref[...], terms=2)
    dec_x = _dot_x01(jnp.broadcast_to(dec, (8, LANES)), selv, terms=2)[0:1, :]
    ke_t = []
    for d in range(2):
        cols = slice(d * ML_QKW, (d + 1) * ML_QKW)
        ke_t.append((refs[d][:, ML_QKW:2 * ML_QKW].astype(F32) * spread[CHUNK:, cols]).T.astype(BF16))
    outs, c_new = [], []
    for ci in range(nch):
        d, h = divmod(ci, ML_HEADS)
        qw = (q[ci].astype(F32) * spread[:CHUNK, ci * ML_QK:(ci + 1) * ML_QK]).astype(BF16)
        outs.append(_dot(s_b[:, ci * CHUNK:(ci + 1) * CHUNK], v[ci]) + _dot(qw, c_old[ci].astype(BF16)))
        c_new.append(dec_x[:, ci * ML_V:(ci + 1) * ML_V] * c_old[ci]
                     + _dot(ke_t[d][h * ML_QK:(h + 1) * ML_QK, :], v[ci]))
    ones = jnp.ones((CHUNK, LANES), BF16)
    n_new = [dec * n_old[d] + jnp.where(nmask_ref[d] > 0, _dot(ke_t[d], ones), 0.0) for d in range(2)]
    return (jnp.concatenate(outs[:ML_HEADS], axis=1), jnp.concatenate(outs[ML_HEADS:], axis=1)), c_new, n_new, m_new


def _mlstm_kernel(qkv_f, g_f, qkv_b, g_b, bias_ref, seli_ref, selv_ref, selq_ref, selit_ref, nmask_ref, hf_ref, hb_ref,
                  c_ref, n_ref, m_ref):
    @pl.when(pl.program_id(1) == 0)
    def _():
        c_ref[...] = jnp.zeros_like(c_ref)
        n_ref[...] = jnp.zeros_like(n_ref)
        m_ref[...] = jnp.zeros_like(m_ref)

    nch = 2 * ML_HEADS
    loaded = [((g_f[bb], g_b[bb]), (qkv_f[bb], qkv_b[bb]), [c_ref[bb * nch + ci] for ci in range(nch)],
               [n_ref[2 * bb + d] for d in range(2)], m_ref[8 * bb:8 * bb + 1, :]) for bb in range(SCAN_ROWS)]
    done = [_mlstm_chunk(*args, bias_ref, seli_ref, selv_ref, selq_ref, selit_ref, nmask_ref) for args in loaded]
    for bb, ((hf, hb), c_new, n_new, m_new) in enumerate(done):
        hf_ref[bb] = hf.astype(hf_ref.dtype)
        hb_ref[bb] = hb.astype(hb_ref.dtype)
        for ci in range(nch):
            c_ref[bb * nch + ci] = c_new[ci]
        for d in range(2):
            n_ref[2 * bb + d] = n_new[d]
        m_ref[8 * bb:8 * bb + 8, :] = jnp.broadcast_to(m_new, (8, LANES))


def _mlstm(qkv, gates, bias_row):
    bsz, t, _ = qkv.shape
    nc, nctx = t // CHUNK, CTX_LEN // CHUNK
    bwd = functools.partial(_chunk_bwd, n_ctx=nctx, n_all=nc)
    w = qkv.shape[2]
    sel = _mlstm_selectors()
    nr = SCAN_ROWS
    assert bsz % nr == 0
    return pl.pallas_call(
        _mlstm_kernel,
        grid=(bsz // nr, nc),
        in_specs=[pl.BlockSpec((nr, CHUNK, w), lambda b, i: (b, i, 0)),
                  pl.BlockSpec((nr, CHUNK, LANES), lambda b, i: (b, i, 0)),
                  pl.BlockSpec((nr, CHUNK, w), lambda b, i: (b, bwd(i), 0)),
                  pl.BlockSpec((nr, CHUNK, LANES), lambda b, i: (b, bwd(i), 0)),
                  pl.BlockSpec((1, LANES), lambda b, i: (0, 0))] + [
                      pl.BlockSpec(a.shape, lambda b, i, n=a.ndim: (0,) * n) for a in sel],
        out_specs=[pl.BlockSpec((nr, CHUNK, ML_W), lambda b, i: (b, i, 0)),
                   pl.BlockSpec((nr, CHUNK, ML_W), lambda b, i: (b, bwd(i), 0))],
        out_shape=[jax.ShapeDtypeStruct((bsz, t, ML_W), BF16)] * 2,
        scratch_shapes=[pltpu.VMEM((nr * 2 * ML_HEADS, ML_QK, ML_V), F32),
                        pltpu.VMEM((nr * 2, ML_QKW, LANES), F32),
                        pltpu.VMEM((nr * 8, LANES), F32)],
        compiler_params=_cparams("arbitrary", "arbitrary"),
        name="mlstm_scan",
    )(qkv, gates, qkv, gates, bias_row, *sel)


def _mlstm_selectors():
    nch = 2 * ML_HEADS
    col = jnp.arange(LANES)[:, None]
    seli = (col == jnp.arange(nch * CHUNK)[None, :] // CHUNK).astype(BF16)
    selv = (col == jnp.arange(nch * ML_V)[None, :] // ML_V).astype(BF16)
    head = (jnp.arange(ML_QKW) // ML_QK)[None, :, None]
    chain = jnp.arange(2)[:, None, None] * ML_HEADS + head
    nmask = (jnp.arange(LANES)[None, None, :] == chain).astype(F32)
    selq = (col == jnp.arange(nch * ML_QK)[None, :] // ML_QK).astype(BF16)
    return seli, selv, selq, seli.T, nmask


def _conv_kernel(x_ref, w_ref, b_ref, o_ref):
    x = x_ref[0]
    t_len = x.shape[0]
    t = lax.broadcasted_iota(jnp.int32, (t_len, 1), 0)
    lo = jnp.where(t < CTX_LEN, 0, CTX_LEN)
    hi = jnp.where(t < CTX_LEN, CTX_LEN, t_len)
    half = SSD_CONV // 2
    acc = x * w_ref[half:half + 1, :]
    for tap in range(SSD_CONV):
        d = tap - half
        if d == 0:
            continue
        shifted = pltpu.roll(x, shift=(-d) % t_len, axis=0)
        ok = (t + d >= lo) & (t + d < hi)
        acc = acc + jnp.where(ok, shifted, 0.0) * w_ref[tap:tap + 1, :]
    o_ref[0] = _silu(acc + b_ref[...])


def _ssd_conv(xbc, w, b):
    bsz, t, n = xbc.shape
    tn = 256
    return pl.pallas_call(
        _conv_kernel,
        grid=(bsz, n // tn),
        in_specs=[pl.BlockSpec((1, t, tn), lambda b, j: (b, 0, j)),
                  pl.BlockSpec((SSD_CONV, tn), lambda b, j: (0, j)),
                  pl.BlockSpec((1, tn), lambda b, j: (0, j))],
        out_specs=pl.BlockSpec((1, t, tn), lambda b, j: (b, 0, j)),
        out_shape=jax.ShapeDtypeStruct((bsz, t, n), F32),
        compiler_params=_cparams("arbitrary", "arbitrary"),
        name="ssd_conv",
    )(xbc, w, b.reshape(1, n))


def _ssd_kernel(x_f, g_f, x_b, g_b, dtb_ref, ea_ref, sel_ref, yf_ref, yb_ref, st_ref):
    @pl.when(pl.program_id(1) == 0)
    def _():
        st_ref[...] = jnp.zeros_like(st_ref)

    lane = lax.broadcasted_iota(jnp.int32, (CHUNK, SSD_W), 1)
    row = lax.broadcasted_iota(jnp.int32, (CHUNK, SSD_W), 0)
    s_idx = lane % SSD_P
    diag = s_idx == row
    ones = jnp.ones((CHUNK, CHUNK), BF16)
    half = lax.broadcasted_iota(jnp.int32, (CHUNK, 2 * SSD_P), 1) < SSD_P
    loads = []
    for bb in range(SCAN_ROWS):
        for d, (x_ref, g_ref, o_ref) in enumerate(((x_f, g_f, yf_ref), (x_b, g_b, yb_ref))):
            loads.append((bb, d, x_ref[bb, :, :SSD_W], x_ref[bb, :, SSD_W:SSD_W + SSD_GN],
                          x_ref[bb, :, SSD_W + SSD_GN:].astype(BF16), g_ref[bb], st_ref[2 * bb + d], o_ref))
    outs = []
    for bb, d, xs, bm_f, cm, gates, st_old, o_ref in loads:
        bm = bm_f.astype(BF16)
        incl, _ = _tri_masks(d)
        mask_t = (s_idx <= row) if d == 0 else (s_idx >= row)
        last = CHUNK - 1 if d == 0 else 0
        dtv = _softplus(gates + dtb_ref[...])
        bcum = _dot_01(incl.astype(BF16), -dtv * ea_ref[...])
        b_col = _dot_x01(bcum, sel_ref[d], terms=2)
        dt_col = _dot_x01(dtv, sel_ref[d], terms=2)
        b_row = _dot_01(ones, jnp.where(diag, b_col, 0.0), terms=2)
        dt_row = _dot_01(ones, jnp.where(diag, dt_col, 0.0), terms=2)
        btot = b_col[last:last + 1, :]
        seg = jnp.exp(jnp.where(mask_t, b_col - b_row, -jnp.inf))
        cb = []
        for g in range(SSD_GROUPS):
            cols = slice(g * SSD_N, (g + 1) * SSD_N)
            cbg = _dot_t(cm[:, cols], bm[:, cols])
            cb.extend([jnp.concatenate([cbg, cbg], axis=1)] * (SSD_HPG // 2))
        w = (jnp.concatenate(cb, axis=1) * seg * dt_row).astype(BF16)
        xsb = xs.astype(BF16)
        st_b = st_old.astype(BF16)
        y_parts = []
        for p in range(SSD_HEADS // 2):
            cols = slice(p * 2 * SSD_P, (p + 1) * 2 * SSD_P)
            xp = xsb[:, cols]
            x_bd = jnp.concatenate([jnp.where(half, xp, 0), jnp.where(half, 0, xp)], axis=0)
            y_parts.append(_dot(w[:, cols], x_bd))
        ys_parts = [_dot(cm[:, g * SSD_N:(g + 1) * SSD_N], st_b[:, g * SSD_HPG * SSD_P:(g + 1) * SSD_HPG * SSD_P])
                    for g in range(SSD_GROUPS)]
        y = jnp.concatenate(y_parts, axis=1) + jnp.exp(b_col) * jnp.concatenate(ys_parts, axis=1)
        xe = (xs * (jnp.exp(btot - b_col) * dt_col)).astype(BF16)
        upd = [_dot(bm_f[:, g * SSD_N:(g + 1) * SSD_N].T.astype(BF16),
                    xe[:, g * SSD_HPG * SSD_P:(g + 1) * SSD_HPG * SSD_P]) for g in range(SSD_GROUPS)]
        outs.append((bb, d, o_ref, y, jnp.exp(btot) * st_old + jnp.concatenate(upd, axis=1)))
    for bb, d, o_ref, y, st_new in outs:
        o_ref[bb] = y.astype(o_ref.dtype)
        st_ref[2 * bb + d] = st_new


def _ssd(xbc, gates, dtb_row, ea_row):
    bsz, t, w = xbc.shape
    nc, nctx = t // CHUNK, CTX_LEN // CHUNK
    bwd = functools.partial(_chunk_bwd, n_ctx=nctx, n_all=nc)
    nr = SCAN_ROWS
    assert bsz % nr == 0
    return pl.pallas_call(
        _ssd_kernel,
        grid=(bsz // nr, nc),
        in_specs=[pl.BlockSpec((nr, CHUNK, w), lambda b, i: (b, i, 0)),
                  pl.BlockSpec((nr, CHUNK, LANES), lambda b, i: (b, i, 0)),
                  pl.BlockSpec((nr, CHUNK, w), lambda b, i: (b, bwd(i), 0)),
                  pl.BlockSpec((nr, CHUNK, LANES), lambda b, i: (b, bwd(i), 0)),
                  pl.BlockSpec((1, LANES), lambda b, i: (0, 0)),
                  pl.BlockSpec((1, LANES), lambda b, i: (0, 0)),
                  pl.BlockSpec((2, LANES, SSD_W), lambda b, i: (0, 0, 0))],
        out_specs=[pl.BlockSpec((nr, CHUNK, SSD_W), lambda b, i: (b, i, 0)),
                   pl.BlockSpec((nr, CHUNK, SSD_W), lambda b, i: (b, bwd(i), 0))],
        out_shape=[jax.ShapeDtypeStruct((bsz, t, SSD_W), BF16)] * 2,
        scratch_shapes=[pltpu.VMEM((2 * nr, SSD_N, SSD_W), F32)],
        compiler_params=_cparams("arbitrary", "arbitrary"),
        name="ssd_scan",
    )(xbc, gates, xbc, gates, dtb_row, ea_row, _ssd_head_select())


def _ssd_head_select():
    col = jnp.arange(LANES)[None, :, None]
    head = (jnp.arange(SSD_W) // SSD_P)[None, None, :]
    d = jnp.arange(2)[:, None, None]
    return (col == 4 * ML_HEADS + d * SSD_HEADS + head).astype(BF16)


def _residual_and_next(y, h_ref, g1_ref, a2_ref, b2_ref, h_out, v_out):
    h_new = h_ref[...] + g1_ref[0] * _rms(y)
    h_out[...] = h_new
    v_out[...] = _rms(h_new) * a2_ref[0] + b2_ref[0]


def _ab_out_kernel(hf_ref, hb_ref, og_ref, yf_ref, yb_ref, xs_ref, z_ref, h_ref, mln_ref, sd_ref, sn_ref,
                   w_ref, g1_ref, a2_ref, b2_ref, h_out, v_out):
    y = None
    for hd in range(ML_HEADS):
        cols = slice(hd * ML_V, (hd + 1) * ML_V)
        hsum = hf_ref[:, cols].astype(F32) + hb_ref[:, cols].astype(F32)
        ml = _rms(hsum) * mln_ref[:, cols] * jax.nn.sigmoid(og_ref[:, cols])
        part = _dot(ml.astype(BF16), w_ref[cols, :])
        y = part if y is None else y + part
    ys = (yf_ref[...].astype(F32) + yb_ref[...].astype(F32) + sd_ref[...] * xs_ref[...]) * _silu(z_ref[...])
    y = y + _dot((_rms(ys) * sn_ref[...]).astype(BF16), w_ref[ML_W:, :])
    _residual_and_next(y, h_ref, g1_ref, a2_ref, b2_ref, h_out, v_out)


def _row_spec(tm, width, col=0):
    return pl.BlockSpec((tm, width), lambda i: (i, col))


def _tab_spec():
    return pl.BlockSpec((1, 1, D_MODEL), lambda i: (i, 0, 0))


def _full_spec(shape):
    return pl.BlockSpec(shape, lambda i: tuple(0 for _ in shape))


def _ab_out(hf, hb, og, yf, yb, xbc, z, h, ml_norm, ssd_d_row, ssd_norm, w_out, g1, a2, b2):
    rows = h.shape[0]
    tm = TOK_BLK
    wide = lambda: _row_spec(tm, D_MODEL)
    return pl.pallas_call(
        _ab_out_kernel,
        grid=(rows // tm,),
        in_specs=[wide(), wide(), wide(), wide(), wide(), wide(), wide(), wide(),
                  _full_spec((1, ML_W)), _full_spec((1, SSD_W)), _full_spec((1, SSD_W)),
                  _full_spec(w_out.shape), _tab_spec(), _tab_spec(), _tab_spec()],
        out_specs=[wide(), wide()],
        out_shape=[jax.ShapeDtypeStruct((rows, D_MODEL), F32)] * 2,
        compiler_params=_cparams("arbitrary"),
        name="ab_out",
    )(hf, hb, og, yf, yb, xbc, z, h, ml_norm, ssd_d_row, ssd_norm, w_out, g1, a2, b2)


def _router_gates_t(x, rw_t, rb_col):
    tm = x.shape[0]
    aff = jax.nn.sigmoid(lax.dot_general(rw_t, x, (((1,), (1,)), ((), ())), precision=HIGHEST,
                                         preferred_element_type=F32))
    sel = aff + rb_col
    row = lambda a, e: a[e:e + 1, :]
    n = EXPERTS_PER_GROUP
    best, gi = None, None
    for g in range(N_GROUPS):
        xs = [row(sel, g * n + j) for j in range(n)]
        score = None
        for a in range(n):
            for b in range(a + 1, n):
                pair = xs[a] + xs[b]
                score = pair if score is None else jnp.maximum(score, pair)
        if g == 0:
            best, gi = score, jnp.zeros((1, tm), jnp.int32)
        else:
            upd = score > best
            gi = jnp.where(upd, g, gi)
            best = jnp.where(upd, score, best)

    def pick(a, j):
        out = row(a, j)
        for g in range(1, N_GROUPS):
            out = jnp.where(gi == g, row(a, g * n + j), out)
        return out

    sel_in = [pick(sel, j) for j in range(n)]
    aff_in = [pick(aff, j) for j in range(n)]

    def argmax_first(vals):
        bv, bi = vals[0], jnp.zeros((1, tm), jnp.int32)
        for j in range(1, n):
            upd = vals[j] > bv
            bi = jnp.where(upd, j, bi)
            bv = jnp.where(upd, vals[j], bv)
        return bi

    i1 = argmax_first(sel_in)
    i2 = argmax_first([jnp.where(i1 == j, -jnp.inf, sel_in[j]) for j in range(n)])

    def take(vals, idx):
        out = vals[0]
        for j in range(1, n):
            out = jnp.where(idx == j, vals[j], out)
        return out

    w1, w2 = take(aff_in, i1), take(aff_in, i2)
    wsum = w1 + w2
    w1, w2 = w1 / wsum, w2 / wsum
    rows = [jnp.ones((1, tm), F32)]
    for e in range(N_EXPERTS):
        g, j = divmod(e, n)
        in_g = jnp.where(i1 == j, w1, 0.0) + jnp.where(i2 == j, w2, 0.0)
        rows.append(jnp.where(gi == g, in_g, 0.0))
    return rows, gi


MOE_TILE = 1024
MOE_CHUNK = 288
MOE_MAX_CHUNKS = -(-MOE_TILE // MOE_CHUNK)
MOE_KEY_COL = N_EXPERTS + 1


def _moe_kernel(v_ref, h_ref, rw_ref, rb_ref, wg_ref, wu_ref, wd_ref, g2_ref, o_ref, acc_ref, gate_ref, xb_ref,
                key_ref, xg_ref, yg_ref, gg_ref, nch_ref, *, nsub):
    e = pl.program_id(1)
    tm, ck = MOE_TILE, MOE_CHUNK

    @pl.when(e == 0)
    def _():
        x = v_ref[...]
        xb = x.astype(BF16)
        xb_ref[...] = xb
        rows, gi = _router_gates_t(x, rw_ref[...], rb_ref[...])
        grp = lax.broadcasted_iota(jnp.int32, (8, tm), 0)
        member = grp == gi
        earlier = (lax.broadcasted_iota(jnp.int32, (tm, tm), 0)
                   < lax.broadcasted_iota(jnp.int32, (tm, tm), 1)).astype(BF16)
        before = _dot(member.astype(BF16), earlier)
        rank = jnp.sum(jnp.where(member, before, 0.0), axis=0, keepdims=True)
        key = gi * tm + rank.astype(jnp.int32)
        key_ref[...] = jnp.broadcast_to(key, key_ref.shape)
        rows = rows + [key.astype(F32), jnp.zeros((LANES - 2 - N_EXPERTS, tm), F32)]
        gate_ref[...] = jnp.concatenate(rows, axis=0).T
        cnt = jnp.sum(member.astype(F32), axis=1, keepdims=True)
        nch = sum(jnp.where(cnt > m * ck, 1, 0) for m in range(MOE_MAX_CHUNKS))
        for g in range(N_GROUPS):
            nch_ref[g] = nch[g, 0]
        act = _silu(_dot(xb, wg_ref[0])) * _dot(xb, wu_ref[0])
        acc_ref[...] = _dot(act.astype(BF16), wd_ref[0])

    g = jnp.maximum(e - 1, 0) // EXPERTS_PER_GROUP
    j = jnp.maximum(e - 1, 0) % EXPERTS_PER_GROUP
    n_chunks = jnp.where(e > 0, nch_ref[g], 0)
    key0 = g * tm
    chunk_rows = lambda c: pl.ds(pl.multiple_of(c * ck, 16), ck)

    def gather(c, carry):
        want = lax.broadcasted_iota(jnp.int32, (ck, tm), 0) + (key0 + c * ck)
        pick = jnp.where(key_ref[0:1, :] == want, 1.0, 0.0).astype(BF16)
        xg_ref[chunk_rows(c), :] = _dot(pick, xb_ref[...]).astype(BF16)
        gg_ref[chunk_rows(c), :] = _dot_01(pick, gate_ref[...], terms=2)
        yg_ref[chunk_rows(c), :] = jnp.zeros((ck, D_MODEL), F32)
        return carry

    lax.fori_loop(0, jnp.where(j == 0, n_chunks, 0), gather, 0)

    def expert(c, carry):
        xc = xg_ref[chunk_rows(c), :]
        lane = lax.broadcasted_iota(jnp.int32, (ck, LANES), 1)
        gate = jnp.sum(jnp.where(lane == e, gg_ref[chunk_rows(c), :], 0.0), axis=1, keepdims=True)
        act = _silu(_dot(xc, wg_ref[0])) * _dot(xc, wu_ref[0]) * gate
        yg_ref[chunk_rows(c), :] += _dot(act.astype(BF16), wd_ref[0])
        return carry

    lax.fori_loop(0, n_chunks, expert, 0)

    def scatter(c, carry):
        want = lax.broadcasted_iota(jnp.int32, (tm, ck), 1) + (key0 + c * ck)
        key_col = gate_ref[:, MOE_KEY_COL:MOE_KEY_COL + 1].astype(jnp.int32)
        place = jnp.where(key_col == want, 1.0, 0.0).astype(BF16)
        acc_ref[...] += _dot(place, yg_ref[chunk_rows(c), :].astype(BF16))
        return carry

    lax.fori_loop(0, jnp.where(j == EXPERTS_PER_GROUP - 1, n_chunks, 0), scatter, 0)

    @pl.when(e == N_EXPERTS)
    def _():
        for s in range(nsub):
            rows = slice(s * TOK_BLK, (s + 1) * TOK_BLK)
            o_ref[rows, :] = h_ref[rows, :] + g2_ref[s] * _rms(acc_ref[rows, :])


def _moe(v, h, rw_t, rb_col, wg, wu, wd, g2):
    rows = v.shape[0]
    tm = MOE_TILE
    nsub = tm // TOK_BLK
    assert rows % tm == 0
    ne = wg.shape[0]
    cap = MOE_MAX_CHUNKS * MOE_CHUNK
    return pl.pallas_call(
        functools.partial(_moe_kernel, nsub=nsub),
        grid=(rows // tm, ne),
        in_specs=[pl.BlockSpec((tm, D_MODEL), lambda i, e: (i, 0)),
                  pl.BlockSpec((tm, D_MODEL), lambda i, e: (i, 0)),
                  pl.BlockSpec(rw_t.shape, lambda i, e: (0, 0)),
                  pl.BlockSpec(rb_col.shape, lambda i, e: (0, 0)),
                  pl.BlockSpec((1, D_MODEL, EXPERT_FF), lambda i, e: (e, 0, 0)),
                  pl.BlockSpec((1, D_MODEL, EXPERT_FF), lambda i, e: (e, 0, 0)),
                  pl.BlockSpec((1, EXPERT_FF, D_MODEL), lambda i, e: (e, 0, 0)),
                  pl.BlockSpec((nsub, 1, D_MODEL), lambda i, e: (i, 0, 0))],
        out_specs=pl.BlockSpec((tm, D_MODEL), lambda i, e: (i, 0)),
        out_shape=jax.ShapeDtypeStruct((rows, D_MODEL), F32),
        scratch_shapes=[pltpu.VMEM((tm, D_MODEL), F32), pltpu.VMEM((tm, LANES), F32),
                        pltpu.VMEM((tm, D_MODEL), BF16), pltpu.VMEM((8, tm), jnp.int32),
                        pltpu.VMEM((cap, D_MODEL), BF16), pltpu.VMEM((cap, D_MODEL), F32),
                        pltpu.VMEM((cap, LANES), F32), pltpu.SMEM((N_GROUPS,), jnp.int32)],
        compiler_params=_cparams("arbitrary", "arbitrary"),
        name="moe",
    )(v, h, rw_t, rb_col, wg, wu, wd, g2)


MLA_KPAD = LANES


def _rope_padded(x, cos, sin):
    n = x.shape[1] // MLA_KPAD
    lane = lax.broadcasted_iota(jnp.int32, (1, x.shape[1]), 1) % MLA_KPAD
    is_x1 = (lane >= MLA_NOPE) & (lane < MLA_NOPE + MLA_ROPE // 2)
    half = MLA_ROPE // 2
    partner = jnp.where(is_x1, pltpu.roll(x, shift=x.shape[1] - half, axis=1), pltpu.roll(x, shift=half, axis=1))
    return x * jnp.tile(cos, (1, n)) + partner * jnp.tile(sin, (1, n))


def _mla_prep_kernel(m_ref, qn_ref, kvn_ref, wq_ref, wk_ref, wv_ref, cos_ref, sin_ref, qt_ref, k_ref, vt_ref):
    cos, sin = cos_ref[...], sin_ref[...]
    ql = (_rms(m_ref[:, :MLA_Q_RANK]) * qn_ref[...]).astype(BF16)
    q = _rope_padded(_dot(ql, wq_ref[...]), cos, sin) * MLA_SCALE
    qt_ref[...] = q.T.astype(qt_ref.dtype)
    kvl = (_rms(m_ref[:, MLA_Q_RANK:MLA_Q_RANK + MLA_KV_RANK]) * kvn_ref[...]).astype(BF16)
    kr = _rope_padded(m_ref[:, MLA_Q_RANK + MLA_KV_RANK:], cos, sin)
    k_ref[...] = (_dot(kvl, wk_ref[...]) + jnp.tile(kr, (1, MLA_HEADS))).astype(k_ref.dtype)
    vt_ref[...] = _dot(kvl, wv_ref[...]).T.astype(vt_ref.dtype)


def _rope_tables(t_len):
    n_lat = t_len - CTX_LEN
    pos = jnp.arange(n_lat)
    n_freq = MLA_ROPE // 4
    inv = ROPE_BASE ** (-jnp.arange(n_freq, dtype=F32) / n_freq)
    ang = jnp.concatenate([(pos // GRID_W).astype(F32)[:, None] * inv, (pos % GRID_W).astype(F32)[:, None] * inv],
                          axis=-1)
    cos = jnp.concatenate([jnp.ones((CTX_LEN, MLA_ROPE // 2), F32), jnp.cos(ang)], axis=0)
    sin = jnp.concatenate([jnp.zeros((CTX_LEN, MLA_ROPE // 2), F32), jnp.sin(ang)], axis=0)
    tail = MLA_KPAD - MLA_NOPE - MLA_ROPE
    cos_t = jnp.concatenate([jnp.ones((t_len, MLA_NOPE), F32), cos, cos, jnp.ones((t_len, tail), F32)], axis=1)
    sin_t = jnp.concatenate([jnp.zeros((t_len, MLA_NOPE), F32), -sin, sin, jnp.zeros((t_len, tail), F32)], axis=1)
    return cos_t, sin_t


def _mla_prep(m, qn, kvn, wq, wk, wv, bsz):
    rows, width = m.shape
    t = rows // bsz
    tm = TOK_BLK
    nb = t // tm
    cos_t, sin_t = _rope_tables(t)
    pos = pl.BlockSpec((tm, MLA_KPAD), lambda i: (i % nb, 0))
    col = lambda n: pl.BlockSpec((n, tm), lambda i: (0, i))
    nk, nv = MLA_HEADS * MLA_KPAD, MLA_HEADS * MLA_V
    return pl.pallas_call(
        _mla_prep_kernel,
        grid=(rows // tm,),
        in_specs=[_row_spec(tm, width), _full_spec(qn.shape), _full_spec(kvn.shape), _full_spec(wq.shape),
                  _full_spec(wk.shape), _full_spec(wv.shape), pos, pos],
        out_specs=[col(nk), _row_spec(tm, nk), col(nv)],
        out_shape=[jax.ShapeDtypeStruct((nk, rows), BF16), jax.ShapeDtypeStruct((rows, nk), BF16),
                   jax.ShapeDtypeStruct((nv, rows), BF16)],
        compiler_params=_cparams("arbitrary"),
        name="mla_prep",
    )(m, qn, kvn, wq, wk, wv, cos_t, sin_t)


def _attn_kernel(qt_ref, k_ref, vt_ref, o_ref):
    def scores(hd):
        return _dot(k_ref[0, :, hd * MLA_KPAD:(hd + 1) * MLA_KPAD], qt_ref[hd * MLA_KPAD:(hd + 1) * MLA_KPAD, :])

    outs = []
    ahead = [scores(0), scores(1)]
    for hd in range(MLA_HEADS):
        s = ahead.pop(0)
        if hd + 2 < MLA_HEADS:
            ahead.append(scores(hd + 2))
        p = jnp.exp(s - jnp.max(s, axis=0, keepdims=True))
        o = _dot(vt_ref[hd * MLA_V:(hd + 1) * MLA_V, :], p.astype(BF16))
        outs.append(o / jnp.sum(p, axis=0, keepdims=True))
    o_ref[...] = jnp.concatenate(outs, axis=0).T.astype(o_ref.dtype)


def _lat_block(i, nb):
    return (i // (nb - 1)) * nb + 1 + i % (nb - 1)


def _attention(qt, k, vt, bsz):
    t = k.shape[1]
    tq = TOK_BLK
    nb = t // tq
    n_lat = bsz * (nb - 1)
    return pl.pallas_call(
        _attn_kernel,
        grid=(n_lat,),
        in_specs=[pl.BlockSpec((qt.shape[0], tq), lambda i: (0, _lat_block(i, nb))),
                  pl.BlockSpec((1, t, k.shape[2]), lambda i: (i // (nb - 1), 0, 0)),
                  pl.BlockSpec((vt.shape[0], t), lambda i: (0, i // (nb - 1)))],
        out_specs=_row_spec(tq, MLA_HEADS * MLA_V),
        out_shape=jax.ShapeDtypeStruct((n_lat * tq, MLA_HEADS * MLA_V), BF16),
        compiler_params=_cparams("arbitrary"),
        name="mla_attention",
    )(qt, k, vt)


def _seg_sum(x):
    i = lax.broadcasted_iota(jnp.int32, (2 * RW_N, 2 * RW_N), 0) // RW_N
    j = lax.broadcasted_iota(jnp.int32, (2 * RW_N, 2 * RW_N), 1) // RW_N
    return _dot_x01(x, (i == j).astype(BF16))


RW_GC = 6


def _rwkv_precompute(it, r_s, v_s, a_s, lw_s, kd_s, bb_s, gam_s, y0_s, phi_s, psi_s, dec_s):
    masks = [_tri_masks(0), _tri_masks(1)]
    hp = 2 * RW_N
    ch = []
    for cc in range(RW_GC):
        c = it * RW_GC + cc
        rows = pl.ds(pl.multiple_of(c * CHUNK, CHUNK), CHUNK)
        rows2 = pl.ds(pl.multiple_of(c * hp, hp), hp)
        r, v, a = r_s[rows, :], v_s[rows, :], a_s[rows, :]
        for d in range(2):
            last = CHUNK - 1 if d == 0 else 0
            lw, kd, bb = lw_s[d, rows, :], kd_s[d, rows, :], bb_s[d, rows, :]
            cl = _dot_01(masks[d][0].astype(BF16), lw)
            cl_tot = cl[last:last + 1, :]
            p_inv, p_end = jnp.exp(-cl), jnp.exp(cl_tot - cl)
            at, rt = a * jnp.exp(cl - lw), r * jnp.exp(cl)
            bt, kt, bh, kh = bb * p_inv, kd * p_inv, bb * p_end, kd * p_end
            dec_s[d, rows2, :] = jnp.broadcast_to(jnp.exp(cl_tot), (hp, hp)).T
            for hh in range(2):
                cols = slice(hh * RW_N, (hh + 1) * RW_N)
                ch.append(dict(
                    d=d, rows=rows, rows2=rows2, at=at[:, cols], rt=rt[:, cols], v=v[:, cols],
                    ar=jnp.concatenate([at[:, cols], rt[:, cols]], axis=0).astype(BF16),
                    bk=jnp.concatenate([bt[:, cols], kt[:, cols]], axis=0).astype(BF16),
                    bkt=jnp.concatenate([bh[:, cols], kh[:, cols]], axis=1).T.astype(BF16)))
    for x in ch:
        x["m4"] = _dot_t(x["ar"], x["bk"])
    for x in ch:
        incl, strict = masks[x["d"]]
        m4 = x.pop("m4")
        x["vb"] = x["v"].astype(BF16)
        x["pw"] = jnp.where(strict, m4[:CHUNK, :CHUNK], 0.0)
        x["aak"] = jnp.where(strict, m4[:CHUNK, CHUNK:], 0.0).astype(BF16)
        x["rb"] = jnp.where(incl, m4[CHUNK:, :CHUNK], 0.0).astype(BF16)
        x["rk"] = jnp.where(incl, m4[CHUNK:, CHUNK:], 0.0).astype(BF16)
    for x in ch:
        x["x"] = jnp.concatenate([_dot(x.pop("aak"), x["vb"]), x.pop("at")], axis=1)
    step = CHUNK // 2
    while True:
        for x in ch:
            x["x"] = x["x"] + _dot(x["pw"].astype(BF16), x["x"].astype(BF16))
        step //= 2
        if step == 0:
            break
        for x in ch:
            pb = x["pw"].astype(BF16)
            x["pw"] = _dot(pb, pb)
    for x in ch:
        xb = x["x"].astype(BF16)
        bkt = x.pop("bkt")
        x["rbx"] = _dot(x.pop("rb"), xb)
        x["rkv"] = _dot(x.pop("rk"), x["vb"])
        x["bx"] = _dot(bkt[:RW_N], xb)
        x["kv"] = _dot(bkt[RW_N:], x["vb"])
    zero = jnp.zeros((RW_N, RW_N), F32)
    lanes = lambda a, b: jnp.concatenate([a, b], axis=1)
    bdiag = lambda a, b: jnp.concatenate([lanes(a, zero), lanes(zero, b)], axis=0)
    for j in range(0, len(ch), 2):
        p0, p1 = ch[j], ch[j + 1]
        d, rows, rows2 = p0["d"], p0["rows"], p0["rows2"]
        gam_s[d, rows, :] = lanes(p0["rt"] + p0["rbx"][:, RW_N:], p1["rt"] + p1["rbx"][:, RW_N:]).astype(BF16)
        y0 = lanes(p0["rbx"][:, :RW_N] + p0["rkv"], p1["rbx"][:, :RW_N] + p1["rkv"])
        if d == 0:
            y0_fwd = y0
        else:
            y0_s[rows, :] = y0_fwd + y0
        phi_s[d, rows2, :] = bdiag(p0["bx"][:, RW_N:], p1["bx"][:, RW_N:]).astype(BF16)
        psi_s[d, rows2, :] = bdiag(p0["bx"][:, :RW_N] + p0["kv"], p1["bx"][:, :RW_N] + p1["kv"])


def _rwkv_step(i, states, phi_s, psi_s, dec_s, st_s, n_chunks, n_ctx):
    hp = 2 * RW_N
    loads = []
    for d in range(2):
        c = i if d == 0 else _chunk_bwd(i, n_ctx, n_chunks)
        rows2 = pl.ds(pl.multiple_of(c * hp, hp), hp)
        loads.append((rows2, phi_s[d, rows2, :], psi_s[d, rows2, :], dec_s[d, rows2, :]))
    new_states = []
    for d in range(2):
        rows2, phi, psi, dec = loads[d]
        hb = states[d].astype(BF16)
        st_s[d, rows2, :] = hb
        new_states.append(states[d] * dec + _dot(phi, hb) + psi)
    return tuple(new_states)


def _rwkv_output(it, gam_s, y0_s, st_s, bon_s, gate_s, ln_ref, o_ref):
    hp = 2 * RW_N
    items = []
    for cc in range(RW_GC):
        c = it * RW_GC + cc
        rows = pl.ds(pl.multiple_of(c * CHUNK, CHUNK), CHUNK)
        rows2 = pl.ds(pl.multiple_of(c * hp, hp), hp)
        items.append(dict(rows=rows, gam=[gam_s[d, rows, :] for d in range(2)],
                          st=[st_s[d, rows2, :] for d in range(2)], y0=y0_s[rows, :],
                          bon=bon_s[rows, :], gate=gate_s[rows, :]))
    for x in items:
        x["o"] = _dot(x["gam"][0], x["st"][0]) + _dot(x["gam"][1], x["st"][1]) + x["y0"]
    for x in items:
        x["ms"] = _seg_sum(x["o"] * x["o"]) * (1.0 / RW_N)
    for x in items:
        y = x["o"] * lax.rsqrt(x["ms"] + EPS) * ln_ref[...] + x["bon"]
        o_ref[0, x["rows"], :] = y * x["gate"]


def _rwkv_kernel(r_ref, k_ref, v_ref, low_ref, mur_ref, muk_ref, muv_ref, mul_ref, w0_ref, w2_ref, a0_ref,
                 a2_ref, g2_ref, kk_ref, ka_ref, rk_ref, ln_ref, o_ref,
                 r_s, v_s, a_s, bon_s, gate_s, lw_s, kd_s, bb_s, gam_s, y0_s, phi_s, psi_s, dec_s, st_s,
                 *, n_chunks, n_ctx):
    t_len = r_s.shape[0]
    t = lax.broadcasted_iota(jnp.int32, (t_len, 1), 0)
    lo = jnp.where(t < CTX_LEN, 0, CTX_LEN)
    hi = jnp.where(t < CTX_LEN, CTX_LEN, t_len)

    def shift_mix(x, mu):
        prev = jnp.where(t - 1 >= lo, pltpu.roll(x, shift=1, axis=0), 0.0)
        nxt = jnp.where(t + 1 < hi, pltpu.roll(x, shift=t_len - 1, axis=0), 0.0)
        return x + mu * (0.5 * (prev + nxt) - x)

    r = shift_mix(r_ref[0], mur_ref[...])
    k = shift_mix(k_ref[0], muk_ref[...])
    v = shift_mix(v_ref[0], muv_ref[...])
    low = shift_mix(low_ref[0], mul_ref[...])
    kk = k * kk_ref[...]
    kk = kk / jnp.maximum(jnp.sqrt(_seg_sum(kk * kk)), 1e-12)
    r_s[...] = r
    v_s[...] = v
    a_s[...] = -kk
    bonus = jnp.zeros_like(r)
    for d in range(2):
        wl = jnp.tanh(low[:, d * RW_DECAY_RANK:(d + 1) * RW_DECAY_RANK])
        al = low[:, 2 * RW_DECAY_RANK + d * RW_A_RANK:2 * RW_DECAY_RANK + (d + 1) * RW_A_RANK]
        wz = w0_ref[d:d + 1, :] + _dot(wl.astype(BF16), w2_ref[d])
        lw_s[d] = -jnp.exp(-_softplus(-wz) - 0.5)
        a = jax.nn.sigmoid(a0_ref[d:d + 1, :] + _dot(al.astype(BF16), a2_ref[d]))
        kd = k * (1.0 + (a - 1.0) * ka_ref[...])
        kd_s[d] = kd
        bb_s[d] = kk * a
        bonus = bonus + _seg_sum(r * kd * rk_ref[...])
    bon_s[...] = bonus * v
    gl = jax.nn.sigmoid(low[:, 2 * RW_DECAY_RANK + 2 * RW_A_RANK:])
    gate_s[...] = _dot(gl.astype(BF16), g2_ref[...])

    def precompute(it, carry):
        _rwkv_precompute(it, r_s, v_s, a_s, lw_s, kd_s, bb_s, gam_s, y0_s, phi_s, psi_s, dec_s)
        return carry

    lax.fori_loop(0, n_chunks // RW_GC, precompute, 0)
    zero = jnp.zeros((2 * RW_N, 2 * RW_N), F32)
    lax.fori_loop(0, n_chunks, lambda i, st: _rwkv_step(i, st, phi_s, psi_s, dec_s, st_s, n_chunks, n_ctx),
                  (zero, zero))

    def output(it, carry):
        _rwkv_output(it, gam_s, y0_s, st_s, bon_s, gate_s, ln_ref, o_ref)
        return carry

    lax.fori_loop(0, n_chunks // RW_GC, output, 0)


def _rwkv(rkv, low, mu, w0, w2, a0, a2, g2, kk, ka, rk, ln):
    bsz, t, _ = rkv.shape
    hp = 2 * RW_N
    n_hp = RW_W // hp
    n_low = low.shape[2]
    tok = lambda off: pl.BlockSpec((1, t, hp), lambda b, j: (b, 0, off * n_hp + j))
    vec = lambda off: pl.BlockSpec((1, hp), lambda b, j: (0, off * n_hp + j))
    par2 = pl.BlockSpec((2, hp), lambda b, j: (0, j))
    par3 = pl.BlockSpec((2, RW_DECAY_RANK, hp), lambda b, j: (0, 0, j))
    tbuf = lambda: pltpu.VMEM((t, hp), F32)
    dbuf = lambda: pltpu.VMEM((2, t, hp), F32)
    mu_rkv, mu_low = mu[None, :3 * RW_W], mu[None, 3 * RW_W:]
    row = lambda a: a.reshape(1, RW_W)
    return pl.pallas_call(
        functools.partial(_rwkv_kernel, n_chunks=t // CHUNK, n_ctx=CTX_LEN // CHUNK),
        grid=(bsz, n_hp),
        in_specs=[tok(0), tok(1), tok(2), pl.BlockSpec((1, t, n_low), lambda b, j: (b, 0, 0)),
                  vec(0), vec(1), vec(2), pl.BlockSpec((1, n_low), lambda b, j: (0, 0)),
                  par2, par3, par2, par3, pl.BlockSpec((RW_GATE_RANK, hp), lambda b, j: (0, j)),
                  vec(0), vec(0), vec(0), vec(0)],
        out_specs=pl.BlockSpec((1, t, hp), lambda b, j: (b, 0, j)),
        out_shape=jax.ShapeDtypeStruct((bsz, t, RW_W), F32),
        scratch_shapes=[tbuf(), tbuf(), tbuf(), tbuf(), tbuf(), dbuf(), dbuf(), dbuf(),
                        pltpu.VMEM((2, t, hp), BF16), tbuf(), pltpu.VMEM((2, 2 * t, hp), BF16), pltpu.VMEM((2, 2 * t, hp), F32),
                        pltpu.VMEM((2, 2 * t, hp), F32), pltpu.VMEM((2, 2 * t, hp), BF16)],
        compiler_params=_cparams("arbitrary", "arbitrary"),
        name="rwkv7",
    )(rkv, rkv, rkv, low, mu_rkv, mu_rkv, mu_rkv, mu_low, w0, w2.astype(BF16), a0, a2.astype(BF16),
      g2.astype(BF16), row(kk), row(ka), row(rk), row(ln))


def _cd_out_kernel(att_ref, rw_ref, h_ref, w_ref, g1_ref, a2_ref, b2_ref, h_out, v_out):
    n_att = MLA_HEADS * MLA_V
    y = _dot(att_ref[...], w_ref[:n_att, :]) + _dot(rw_ref[...].astype(BF16), w_ref[n_att:, :])
    _residual_and_next(y, h_ref, g1_ref, a2_ref, b2_ref, h_out, v_out)


def _cd_out(att, rw, h, w_out, g1, a2, b2, nb):
    rows = att.shape[0]
    tm = TOK_BLK
    lat = lambda: pl.BlockSpec((tm, D_MODEL), lambda i: (_lat_block(i, nb), 0))
    return pl.pallas_call(
        _cd_out_kernel,
        grid=(rows // tm,),
        in_specs=[_row_spec(tm, att.shape[1]), lat(), lat(), _full_spec(w_out.shape), _tab_spec(), _tab_spec(),
                  _tab_spec()],
        out_specs=[_row_spec(tm, D_MODEL), _row_spec(tm, D_MODEL)],
        out_shape=[jax.ShapeDtypeStruct((rows, D_MODEL), F32)] * 2,
        compiler_params=_cparams("arbitrary"),
        name="cd_out",
    )(att, rw, h, w_out, g1, a2, b2)


def _pad_cols(w, width):
    return jnp.pad(w, ((0, 0), (0, width - w.shape[1])))


def _split_cols(w, sizes):
    assert sum(sizes) == w.shape[1]
    out, start = [], 0
    for s in sizes:
        out.append(w[:, start:start + s])
        start += s
    return out


def _layer_ab(h, tabs, ab_w_in, ab_w_out, ml_i_bias, ml_f_bias, ml_norm, ssd_conv_w, ssd_conv_b, ssd_dt_bias,
              ssd_a_log, ssd_d, ssd_norm, bsz):
    a1, b1, g1, a2, b2, _ = tabs
    t = h.shape[0] // bsz
    q, k, v, og, ig, fg, z, xbc, dt = _split_cols(
        ab_w_in, (ML_QKW, ML_QKW, ML_W, ML_W, 2 * ML_HEADS, 2 * ML_HEADS, SSD_W, SSD_XBC, 2 * SSD_HEADS))
    small = _pad_cols(jnp.concatenate([ig, fg, dt], axis=1), LANES)
    w_in = jnp.concatenate([q, k, v, og, z, xbc, small], axis=1).astype(BF16)
    widths = (2 * ML_QKW + ML_W, ML_W, SSD_W, SSD_XBC, LANES)
    starts = [0]
    for wd in widths[:-1]:
        starts.append(starts[-1] + wd)
    qkv, og_p, z_p, xbc_p, small_p = _norm_proj(h, a1, b1, w_in, tuple(zip(starts, widths)),
                                               (BF16, F32, F32, F32, F32), "ab_in_proj")
    pad = LANES - 4 * ML_HEADS
    bias_row = jnp.concatenate([ml_i_bias.reshape(-1), ml_f_bias.reshape(-1), jnp.zeros((pad,), F32)])[None]
    hf, hb = _mlstm(qkv.reshape(bsz, t, -1), small_p.reshape(bsz, t, LANES), bias_row)
    xbc_c = _ssd_conv(xbc_p.reshape(bsz, t, SSD_XBC), ssd_conv_w, ssd_conv_b)
    lead = jnp.zeros((4 * ML_HEADS,), F32)
    tail = jnp.zeros((LANES - 4 * ML_HEADS - 2 * SSD_HEADS,), F32)
    dtb_row = jnp.concatenate([lead, ssd_dt_bias.reshape(-1), tail])[None]
    ea_row = jnp.concatenate([lead, jnp.exp(ssd_a_log).reshape(-1), tail])[None]
    yf, yb = _ssd(xbc_c, small_p.reshape(bsz, t, LANES), dtb_row, ea_row)
    flat = lambda a: a.reshape(bsz * t, -1)
    return _ab_out(flat(hf), flat(hb), og_p, flat(yf), flat(yb), flat(xbc_c), z_p, h, ml_norm[None],
                   jnp.repeat(ssd_d, SSD_P)[None], ssd_norm[None], ab_w_out.astype(BF16), g1, a2, b2)


def _pad_heads(w, n_heads, take, width):
    k, n = w.shape
    w3 = w.reshape(k, n_heads, n // n_heads)[:, :, take]
    return jnp.pad(w3, ((0, 0), (0, 0), (0, width - w3.shape[2]))).reshape(k, n_heads * width)


def _layer_cd(h, tabs_all, tabs_lat, cd_w_in, cd_w_out, mla_qn, mla_w_uq, mla_kvn, mla_w_ukv, rw_mu, rw_w0, rw_w2,
              rw_a0, rw_a2, rw_g2, rw_kk, rw_ka, rw_rk, rw_ln, bsz):
    a1, b1 = tabs_all[0], tabs_all[1]
    _, _, g1, a2, b2, _ = tabs_lat
    t = h.shape[0] // bsz
    n_lat = MLA_Q_RANK + MLA_KV_RANK
    n_mla = n_lat + MLA_ROPE
    n_low = 2 * RW_DECAY_RANK + 2 * RW_A_RANK + RW_GATE_RANK
    mla_w = n_lat + MLA_KPAD
    k_rope = jnp.pad(cd_w_in[:, n_lat:n_mla], ((0, 0), (MLA_NOPE, MLA_KPAD - MLA_NOPE - MLA_ROPE)))
    w_in = jnp.concatenate([cd_w_in[:, :n_lat], k_rope, cd_w_in[:, n_mla:]], axis=1).astype(BF16)
    segs = ((0, mla_w), (mla_w, 3 * RW_W), (mla_w + 3 * RW_W, n_low))
    m, rkv, low = _norm_proj(h, a1, b1, w_in, segs, (F32, F32, F32), "cd_in_proj")
    wq = _pad_heads(mla_w_uq, MLA_HEADS, slice(None), MLA_KPAD).astype(BF16)
    wk = _pad_heads(mla_w_ukv, MLA_HEADS, slice(0, MLA_NOPE), MLA_KPAD).astype(BF16)
    wv = _pad_heads(mla_w_ukv, MLA_HEADS, slice(MLA_NOPE, None), MLA_V).astype(BF16)
    qt, k, vt = _mla_prep(m, mla_qn[None], mla_kvn[None], wq, wk, wv, bsz)
    per_b = lambda a: a.reshape(bsz, t, -1)
    att = _attention(qt, per_b(k), vt, bsz)
    rw = _rwkv(per_b(rkv), per_b(low), rw_mu, rw_w0, rw_w2, rw_a0, rw_a2, rw_g2, rw_kk, rw_ka, rw_rk, rw_ln)
    return _cd_out(att, rw.reshape(bsz * t, RW_W), h, cd_w_out.astype(BF16), g1, a2, b2, t // TOK_BLK)


def _moe_weights(router_w, router_bias, exp_w_gate, exp_w_up, exp_w_down, sh_w_gate, sh_w_up, sh_w_down):
    stack = lambda s, e: jnp.concatenate([s[None], e], axis=0).astype(BF16)
    return (router_w.T, router_bias[:, None], stack(sh_w_gate, exp_w_gate), stack(sh_w_up, exp_w_up),
            stack(sh_w_down, exp_w_down))


def kernel(x, c, ctx, c_ctx, w_mod, b_mod, norm_g, ab_w_in, ab_w_out, ml_i_bias, ml_f_bias, ml_norm, ssd_conv_w, ssd_conv_b, ssd_dt_bias, ssd_a_log, ssd_d, ssd_norm, cd_w_in, cd_w_out, mla_qn, mla_w_uq, mla_kvn, mla_w_ukv, rw_mu, rw_w0, rw_w2, rw_a0, rw_a2, rw_g2, rw_kk, rw_ka, rw_rk, rw_ln, router_w, router_bias, exp_w_gate, exp_w_up, exp_w_down, sh_w_gate, sh_w_up, sh_w_down):
    bsz, seq, _ = x.shape
    t = CTX_LEN + seq
    h = jnp.concatenate([ctx, x], axis=1).reshape(bsz * t, D_MODEL)

    mod = _modulation(c, c_ctx, w_mod[0], b_mod[0])
    tabs = _block_tables(mod, norm_g[0], bsz, t // TOK_BLK, True)
    h, v = _layer_ab(h, tabs, ab_w_in[0], ab_w_out[0], ml_i_bias[0], ml_f_bias[0], ml_norm[0], ssd_conv_w[0],
                     ssd_conv_b[0], ssd_dt_bias[0], ssd_a_log[0], ssd_d[0], ssd_norm[0], bsz)
    moe_w = _moe_weights(router_w, router_bias, exp_w_gate[0], exp_w_up[0], exp_w_down[0], sh_w_gate[0],
                         sh_w_up[0], sh_w_down[0])
    h = _moe(v, h, *moe_w, tabs[5])

    mod = _modulation(c, c_ctx, w_mod[1], b_mod[1])
    nb = t // TOK_BLK
    tabs_all = _block_tables(mod, norm_g[1], bsz, nb, True)
    tabs_lat = _block_tables(mod, norm_g[1], bsz, nb - 1, False)
    h, v = _layer_cd(h, tabs_all, tabs_lat, cd_w_in[0], cd_w_out[0], mla_qn[0], mla_w_uq[0], mla_kvn[0],
                     mla_w_ukv[0], rw_mu[0], rw_w0[0], rw_w2[0], rw_a0[0], rw_a2[0], rw_g2[0], rw_kk[0], rw_ka[0],
                     rw_rk[0], rw_ln[0], bsz)
    moe_w = _moe_weights(router_w, router_bias, exp_w_gate[1], exp_w_up[1], exp_w_down[1], sh_w_gate[1],
                         sh_w_up[1], sh_w_down[1])
    h = _moe(v, h, *moe_w, tabs_lat[5])
    return h.reshape(bsz, seq, D_MODEL)
```
